```python
import jax, jax.numpy as jnp
from jax import lax
import numpy as np


D_MODEL = 1024
BATCH = 16
SEQ = 4096
DEPTH = 1

CHUNK = 64
Q_BLOCK = 128
GLA_HEADS = 4
GLA_DK = 64
GLA_DV = 128
GLA_LOWRANK = 16
GLA_TAU = 16.0
FOX_HEADS = 8
FOX_HD = 64
D_FF = 2816
CONV_W = 3
EPS = 1e-6

GLA_WIDTH = GLA_HEADS * GLA_DV
FOX_WIDTH = FOX_HEADS * FOX_HD
IN_SPLITS = (GLA_HEADS * GLA_DK, GLA_HEADS * GLA_DK, GLA_WIDTH, GLA_WIDTH, GLA_LOWRANK,
             FOX_WIDTH, FOX_WIDTH, FOX_WIDTH, FOX_HEADS, D_MODEL, D_MODEL)
IN_COLS = sum(IN_SPLITS)
IN_OFFSETS = tuple(int(v) for v in np.cumsum(IN_SPLITS)[:-1])

kernel_name = 'hybrid_gla_fox_convffn_block'


def rmsnorm(x, g):
    xf = x.astype(jnp.float32)
    y = xf * lax.rsqrt(jnp.mean(xf * xf, axis=-1, keepdims=True) + EPS)
    return (y * g.astype(jnp.float32)).astype(x.dtype)


def gla_chunked(q, k, v, log_a):
    B, S, H, dk = q.shape
    dv = v.shape[-1]
    n = S // CHUNK
    def blocks(t):
        return t.astype(jnp.float32).reshape(B, n, CHUNK, H, t.shape[-1]).transpose(0, 3, 1, 2, 4)
    qc, kc, vc, la = blocks(q), blocks(k), blocks(v), blocks(log_a)
    b = jnp.cumsum(la, axis=3)
    q_e = qc * jnp.exp(b)
    k_e = kc * jnp.exp(-b)
    causal = jnp.tril(jnp.ones((CHUNK, CHUNK), dtype=bool))
    scores = jnp.where(causal, jnp.einsum('bhnld,bhnmd->bhnlm', q_e, k_e), 0.0)
    o_intra = jnp.einsum('bhnlm,bhnmv->bhnlv', scores, vc)
    b_last = b[:, :, :, -1:, :]
    chunk_kv = jnp.einsum('bhnld,bhnlv->bhndv', kc * jnp.exp(b_last - b), vc)
    decay = jnp.exp(b_last[:, :, :, 0, :])

    def step(state, inp):
        dec, kv = inp
        return dec[..., None] * state + kv, state

    _, prev = lax.scan(step, jnp.zeros((B, H, dk, dv), jnp.float32),
                       (jnp.moveaxis(decay, 2, 0), jnp.moveaxis(chunk_kv, 2, 0)))
    prev = jnp.moveaxis(prev, 0, 2)
    o = o_intra + jnp.einsum('bhnld,bhndv->bhnlv', q_e, prev)
    return o.transpose(0, 2, 3, 1, 4).reshape(B, S, H, dv).astype(v.dtype)


def fox_attention(q, k, v, log_f):
    B, S, H, d = q.shape
    scale = d ** -0.5
    qh, kh, vh = (t.transpose(0, 2, 1, 3) for t in (q, k, v))
    c = jnp.cumsum(log_f.astype(jnp.float32), axis=1).transpose(0, 2, 1)
    outs = []
    for i in range(S // Q_BLOCK):
        lo, hi = i * Q_BLOCK, (i + 1) * Q_BLOCK
        logits = jnp.einsum('bhqd,bhkd->bhqk', qh[:, :, lo:hi], kh[:, :, :hi]).astype(jnp.float32) * scale
        logits = logits + c[:, :, lo:hi, None] - c[:, :, None, :hi]
        mask = jnp.arange(hi)[None, :] <= jnp.arange(lo, hi)[:, None]
        p = jax.nn.softmax(jnp.where(mask, logits, -jnp.inf), axis=-1)
        outs.append(jnp.einsum('bhqk,bhkd->bhqd', p.astype(v.dtype), vh[:, :, :hi]))
    return jnp.concatenate(outs, axis=2).transpose(0, 2, 1, 3)


def causal_dwconv(u, w, bias):
    S = u.shape[1]
    up = jnp.pad(u, ((0, 0), (CONV_W - 1, 0), (0, 0)))
    out = bias
    for j in range(CONV_W):
        out = out + w[j] * up[:, j:j + S]
    return out


def hybrid_layer(x, norm_mix_g, w_in, gla_alpha_w2, gla_alpha_b, gla_out_norm_g,
                 fox_forget_b, fox_q_norm_g, fox_k_norm_g, gate_b, w_gla_branch,
                 w_fox_branch, w_out, norm_ffn_g, w_up, conv_w, conv_b, w_down):
    B, S, _ = x.shape
    h = rmsnorm(x, norm_mix_g)
    gq, gk, gv, gr, glr, fq, fk, fv, ff, g_gla, g_fox = jnp.split(h @ w_in, IN_OFFSETS, axis=-1)

    log_a = jax.nn.log_sigmoid(glr @ gla_alpha_w2 + gla_alpha_b) / GLA_TAU
    o_gla = gla_chunked(gq.reshape(B, S, GLA_HEADS, GLA_DK) * (GLA_DK ** -0.5),
                        gk.reshape(B, S, GLA_HEADS, GLA_DK),
                        gv.reshape(B, S, GLA_HEADS, GLA_DV),
                        log_a.reshape(B, S, GLA_HEADS, GLA_DK))
    o_gla = rmsnorm(o_gla, gla_out_norm_g.reshape(GLA_HEADS, GLA_DV)).reshape(B, S, GLA_WIDTH) * jax.nn.silu(gr)

    fqh = rmsnorm(fq.reshape(B, S, FOX_HEADS, FOX_HD), fox_q_norm_g)
    fkh = rmsnorm(fk.reshape(B, S, FOX_HEADS, FOX_HD), fox_k_norm_g)
    log_f = jax.nn.log_sigmoid(ff + fox_forget_b)
    o_fox = fox_attention(fqh, fkh, fv.reshape(B, S, FOX_HEADS, FOX_HD), log_f).reshape(B, S, FOX_WIDTH)

    y = jax.nn.sigmoid(g_gla + gate_b[0]) * (o_gla @ w_gla_branch) \
        + jax.nn.sigmoid(g_fox + gate_b[1]) * (o_fox @ w_fox_branch)
    x = x + y @ w_out

    a, val = jnp.split(rmsnorm(x, norm_ffn_g) @ w_up, 2, axis=-1)
    a = causal_dwconv(a, conv_w, conv_b)
    return x + (jax.nn.gelu(a) * val) @ w_down


def setup_inputs(seed: int = 0) -> dict:
    key = jax.random.key(seed)
    ks = jax.random.split(key, 20)
    L = DEPTH
    nrm = jax.random.normal
    return {
        'x': nrm(ks[0], (BATCH, SEQ, D_MODEL), jnp.float32),
        'norm_mix_g': 1.0 + 0.02 * nrm(ks[1], (L, D_MODEL), jnp.float32),
        'w_in': nrm(ks[2], (L, D_MODEL, IN_COLS), jnp.float32) * D_MODEL ** -0.5,
        'gla_alpha_w2': nrm(ks[3], (L, GLA_LOWRANK, GLA_HEADS * GLA_DK), jnp.float32) * GLA_LOWRANK ** -0.5,
        'gla_alpha_b': 0.1 * nrm(ks[4], (L, GLA_HEADS * GLA_DK), jnp.float32),
        'gla_out_norm_g': 1.0 + 0.02 * nrm(ks[5], (L, GLA_WIDTH), jnp.float32),
        'fox_forget_b': 3.0 + nrm(ks[6], (L, FOX_HEADS), jnp.float32),
        'fox_q_norm_g': 1.0 + 0.02 * nrm(ks[7], (L, FOX_HD), jnp.float32),
        'fox_k_norm_g': 1.0 + 0.02 * nrm(ks[8], (L, FOX_HD), jnp.float32),
        'gate_b': 0.1 * nrm(ks[9], (L, 2, D_MODEL), jnp.float32),
        'w_gla_branch': nrm(ks[10], (L, GLA_WIDTH, D_MODEL), jnp.float32) * GLA_WIDTH ** -0.5,
        'w_fox_branch': nrm(ks[11], (L, FOX_WIDTH, D_MODEL), jnp.float32) * FOX_WIDTH ** -0.5,
        'w_out': nrm(ks[12], (L, D_MODEL, D_MODEL), jnp.float32) * D_MODEL ** -0.5,
        'norm_ffn_g': 1.0 + 0.02 * nrm(ks[13], (L, D_MODEL), jnp.float32),
        'w_up': nrm(ks[14], (L, D_MODEL, 2 * D_FF), jnp.float32) * D_MODEL ** -0.5,
        'conv_w': nrm(ks[15], (L, CONV_W, D_FF), jnp.float32) * CONV_W ** -0.5,
        'conv_b': 0.02 * nrm(ks[16], (L, D_FF), jnp.float32),
        'w_down': nrm(ks[17], (L, D_FF, D_MODEL), jnp.float32) * D_FF ** -0.5,
        'norm_final_g': 1.0 + 0.02 * nrm(ks[18], (D_MODEL,), jnp.float32),
    }


def reference(x, norm_mix_g, w_in, gla_alpha_w2, gla_alpha_b, gla_out_norm_g,
              fox_forget_b, fox_q_norm_g, fox_k_norm_g, gate_b, w_gla_branch,
              w_fox_branch, w_out, norm_ffn_g, w_up, conv_w, conv_b, w_down, norm_final_g):
    for l in range(DEPTH):
        x = hybrid_layer(x, norm_mix_g[l], w_in[l], gla_alpha_w2[l], gla_alpha_b[l], gla_out_norm_g[l],
                         fox_forget_b[l], fox_q_norm_g[l], fox_k_norm_g[l], gate_b[l], w_gla_branch[l],
                         w_fox_branch[l], w_out[l], norm_ffn_g[l], w_up[l], conv_w[l], conv_b[l], w_down[l])
    return rmsnorm(x, norm_final_g)
```

```python
import functools

import jax
import jax.numpy as jnp
from jax import lax
from jax.experimental import pallas as pl
from jax.experimental.pallas import tpu as pltpu

D_MODEL = 1024
CHUNK = 64
GLA_HEADS = 4
GLA_DK = 64
GLA_DV = 128
GLA_LOWRANK = 16
GLA_TAU = 16.0
FOX_HEADS = 8
FOX_HD = 64
D_FF = 2816
CONV_W = 3
EPS = 1e-6

GLA_KW = GLA_HEADS * GLA_DK
GLA_WIDTH = GLA_HEADS * GLA_DV
FOX_WIDTH = FOX_HEADS * FOX_HD
IN_SPLITS = (GLA_KW, GLA_KW, GLA_WIDTH, GLA_WIDTH, GLA_LOWRANK,
             FOX_WIDTH, FOX_WIDTH, FOX_WIDTH, FOX_HEADS, D_MODEL, D_MODEL)

LANES = 128
SUBLANES = 8
MXU_N = 256
VMEM_LIMIT = 56 * 1024 * 1024

SEC_GLA = 2 * GLA_KW + 2 * GLA_WIDTH
SEC_FOX = 3 * FOX_WIDTH
OFF_GLA = 0
OFF_LR = OFF_GLA + SEC_GLA
OFF_FOX = OFF_LR + LANES
OFF_FF = OFF_FOX + SEC_FOX
OFF_GATE = OFF_FF + LANES
PACKED_COLS = OFF_GATE + 2 * D_MODEL

FF_CHUNK = MXU_N
N_FF_CHUNKS = D_FF // FF_CHUNK

BF16 = jnp.bfloat16
F32 = jnp.float32
NEG = -1e30


def _dot(a, b):
    return jnp.dot(a, b, preferred_element_type=F32)


def _dot_nt(a, b):
    return lax.dot_general(a, b, (((1,), (1,)), ((), ())), preferred_element_type=F32)


def _dot_tn(a, b):
    return lax.dot_general(a, b, (((0,), (0,)), ((), ())), preferred_element_type=F32)


def _rms(x, g):
    return x * lax.rsqrt(jnp.mean(x * x, axis=-1, keepdims=True) + EPS) * g


def _log_sigmoid(z):
    return jnp.minimum(z, 0.0) - jnp.log1p(jnp.exp(-jnp.abs(z)))


def _split3(a):
    hi = a.astype(BF16)
    r1 = a - hi.astype(F32)
    mid = r1.astype(BF16)
    lo = (r1 - mid.astype(F32)).astype(BF16)
    return hi, mid, lo


def _cumsum_rows(tri, a):
    hi, mid, lo = _split3(a)
    return _dot(tri, hi) + _dot(tri, mid) + _dot(tri, lo)


def _lower_tri(n):
    r = lax.broadcasted_iota(jnp.int32, (n, n), 0)
    c = lax.broadcasted_iota(jnp.int32, (n, n), 1)
    return r >= c


def _proj_kernel(x_ref, g_ref, w_ref, a2_ref, ab_ref, fb_ref, qg_ref, kg_ref, gb_ref, bd_ref,
                 gla_ref, la_ref, fox_ref, c_ref, ct_ref, gate_ref, carry_ref):
    tm = x_ref.shape[0]

    @pl.when(pl.program_id(1) == 0)
    def _():
        carry_ref[...] = jnp.zeros_like(carry_ref)

    h = _rms(x_ref[...], g_ref[...]).astype(BF16)

    gla_ref[...] = _dot(h, w_ref[:, OFF_GLA:OFF_GLA + SEC_GLA]).astype(BF16)

    glr = _dot(h, w_ref[:, OFF_LR:OFF_LR + LANES])
    z = jnp.dot(glr, a2_ref[...], preferred_element_type=F32,
                precision=lax.Precision.HIGHEST) + ab_ref[...]
    la_ref[...] = _log_sigmoid(z) * (1.0 / GLA_TAU)

    f = _dot(h, w_ref[:, OFF_FOX:OFF_FOX + SEC_FOX])
    bd = bd_ref[...]

    def head_norm(t, g):
        ms = _dot((t * t).astype(BF16), bd) * (1.0 / FOX_HD)
        return t * lax.rsqrt(ms + EPS) * g

    fox_ref[:, 0:FOX_WIDTH] = head_norm(f[:, 0:FOX_WIDTH], qg_ref[...]).astype(BF16)
    fox_ref[:, FOX_WIDTH:2 * FOX_WIDTH] = head_norm(f[:, FOX_WIDTH:2 * FOX_WIDTH], kg_ref[...]).astype(BF16)
    fox_ref[:, 2 * FOX_WIDTH:] = f[:, 2 * FOX_WIDTH:].astype(BF16)

    lf = _log_sigmoid(_dot(h, w_ref[:, OFF_FF:OFF_FF + LANES]) + fb_ref[...])
    tri = _lower_tri(tm).astype(BF16)
    c = _cumsum_rows(tri, lf) + carry_ref[...]
    carry_ref[...] = c[tm - 1:tm, :]
    c_ref[...] = c
    ct_ref[0] = jnp.transpose(c)[0:FOX_HEADS, :]

    gate_ref[...] = jax.nn.sigmoid(_dot(h, w_ref[:, OFF_GATE:]) + gb_ref[...]).astype(BF16)


def _proj_call(x2, g, w, a2, ab, fb, qg, kg, gb, bd, batch, seq, tm):
    n = seq // tm
    rows = lambda b, i: (b * n + i, 0)
    const = lambda b, i: (0, 0)

    def resident(shape):
        return pl.BlockSpec(shape, const, pipeline_mode=pl.Buffered(1))

    tokens = batch * seq
    return pl.pallas_call(
        _proj_kernel,
        grid=(batch, n),
        in_specs=[
            pl.BlockSpec((tm, D_MODEL), rows),
            resident((1, D_MODEL)),
            resident((D_MODEL, PACKED_COLS)),
            resident((LANES, GLA_KW)),
            resident((1, GLA_KW)),
            resident((1, LANES)),
            resident((1, FOX_WIDTH)),
            resident((1, FOX_WIDTH)),
            resident((1, 2 * D_MODEL)),
            resident((FOX_WIDTH, FOX_WIDTH)),
        ],
        out_specs=[
            pl.BlockSpec((tm, SEC_GLA), rows),
            pl.BlockSpec((tm, GLA_KW), rows),
            pl.BlockSpec((tm, SEC_FOX), rows),
            pl.BlockSpec((tm, LANES), rows),
            pl.BlockSpec((1, FOX_HEADS, tm), lambda b, i: (b, 0, i)),
            pl.BlockSpec((tm, 2 * D_MODEL), rows),
        ],
        out_shape=[
            jax.ShapeDtypeStruct((tokens, SEC_GLA), BF16),
            jax.ShapeDtypeStruct((tokens, GLA_KW), F32),
            jax.ShapeDtypeStruct((tokens, SEC_FOX), BF16),
            jax.ShapeDtypeStruct((tokens, LANES), F32),
            jax.ShapeDtypeStruct((batch, FOX_HEADS, seq), F32),
            jax.ShapeDtypeStruct((tokens, 2 * D_MODEL), BF16),
        ],
        scratch_shapes=[pltpu.VMEM((1, LANES), F32)],
        compiler_params=pltpu.CompilerParams(
            dimension_semantics=("arbitrary", "arbitrary"), vmem_limit_bytes=VMEM_LIMIT),
        name="proj",
    )(x2, g, w, a2, ab, fb, qg, kg, gb, bd)


def _gla_kernel(gin_ref, la_ref, ng_ref, o_ref, st_ref):
    tm = gin_ref.shape[0]

    @pl.when(pl.program_id(1) == 0)
    def _():
        st_ref[...] = jnp.zeros_like(st_ref)

    causal = _lower_tri(CHUNK)
    tri = causal.astype(BF16)
    lane_head = lax.broadcasted_iota(jnp.int32, (1, GLA_KW), 1) // GLA_DK
    st_r = lax.broadcasted_iota(jnp.int32, (GLA_KW, GLA_WIDTH), 0) // GLA_DK
    st_c = lax.broadcasted_iota(jnp.int32, (GLA_KW, GLA_WIDTH), 1) // GLA_DV
    on_diag = st_r == st_c
    ng = ng_ref[...]

    for ci in range(tm // CHUNK):
        rows = slice(ci * CHUNK, (ci + 1) * CHUNK)
        b = _cumsum_rows(tri, la_ref[rows, :])
        b_last = b[CHUNK - 1:CHUNK, :]
        q = gin_ref[rows, 0:GLA_KW].astype(F32) * (GLA_DK ** -0.5)
        k = gin_ref[rows, GLA_KW:2 * GLA_KW].astype(F32)
        v = gin_ref[rows, 2 * GLA_KW:2 * GLA_KW + GLA_WIDTH]
        r = gin_ref[rows, 2 * GLA_KW + GLA_WIDTH:].astype(F32)
        q_e = (q * jnp.exp(b)).astype(BF16)
        k_e = (k * jnp.exp(-b)).astype(BF16)
        k_d = (k * jnp.exp(b_last - b)).astype(BF16)

        state = st_ref[...]
        o_inter = _dot(q_e, state.astype(BF16))
        parts = []
        for hd in range(GLA_HEADS):
            q_h = jnp.where(lane_head == hd, q_e, jnp.zeros_like(q_e))
            s = jnp.where(causal, _dot_nt(q_h, k_e), 0.0).astype(BF16)
            parts.append(_dot(s, v[:, hd * GLA_DV:(hd + 1) * GLA_DV]))
        o = jnp.concatenate(parts, axis=1) + o_inter

        kv = _dot_tn(k_d, v)
        dec = jnp.transpose(jnp.broadcast_to(jnp.exp(b_last), (LANES, GLA_KW)))
        dec = jnp.concatenate([dec] * (GLA_WIDTH // LANES), axis=1)
        st_ref[...] = dec * state + jnp.where(on_diag, kv, 0.0)

        outs = []
        for hd in range(GLA_HEADS):
            sl = slice(hd * GLA_DV, (hd + 1) * GLA_DV)
            outs.append(_rms(o[:, sl], ng[:, sl]))
        o_ref[rows, :] = (jnp.concatenate(outs, axis=1) * (r * jax.nn.sigmoid(r))).astype(o_ref.dtype)


def _gla_call(gin, la, ng, batch, seq, tm):
    n = seq // tm
    rows = lambda b, i: (b * n + i, 0)
    return pl.pallas_call(
        _gla_kernel,
        grid=(batch, n),
        in_specs=[
            pl.BlockSpec((tm, SEC_GLA), rows),
            pl.BlockSpec((tm, GLA_KW), rows),
            pl.BlockSpec((1, GLA_WIDTH), lambda b, i: (0, 0)),
        ],
        out_specs=pl.BlockSpec((tm, GLA_WIDTH), rows),
        out_shape=jax.ShapeDtypeStruct((batch * seq, GLA_WIDTH), BF16),
        scratch_shapes=[pltpu.VMEM((GLA_KW, GLA_WIDTH), F32)],
        compiler_params=pltpu.CompilerParams(
            dimension_semantics=("arbitrary", "arbitrary"), vmem_limit_bytes=VMEM_LIMIT),
        name="gla",
    )(gin, la, ng)


def _fox_kernel(q_ref, k_ref, v_ref, c_ref, ct_ref, o_ref, *, blk):
    hp = pl.program_id(1)
    seq = q_ref.shape[0]
    lane = lax.broadcasted_iota(jnp.int32, (1, LANES), 1)
    causal = _lower_tri(blk)

    def q_body(qi, carry):
        q0 = pl.multiple_of(qi * blk, blk)
        q2 = q_ref[pl.ds(q0, blk), :]
        c2 = c_ref[pl.ds(q0, blk), :]
        outs = []
        for j in range(2):
            hd = 2 * hp + j
            in_head = (lane >= j * FOX_HD) & (lane < (j + 1) * FOX_HD)
            q_h = jnp.where(in_head, q2, jnp.zeros_like(q2))
            cq = jnp.sum(jnp.where(lane == hd, c2, 0.0), axis=1, keepdims=True)

            def kv_step(ki, m, l, acc, masked):
                k0 = pl.multiple_of(ki * blk, blk)
                s = _dot_nt(q_h, k_ref[pl.ds(k0, blk), :])
                s = s - ct_ref[0, hd, pl.ds(ki, 1), :]
                if masked:
                    s = jnp.where(causal, s, NEG)
                m_new = jnp.maximum(m, jnp.max(s, axis=1, keepdims=True) + cq)
                p = jnp.exp(s - (m_new - cq))
                alpha = jnp.exp(m - m_new)
                l = alpha * l + jnp.sum(p, axis=1, keepdims=True)
                acc = alpha * acc + _dot(p.astype(BF16), v_ref[pl.ds(k0, blk), :])
                return m_new, l, acc

            init = (jnp.full((blk, 1), NEG, F32), jnp.zeros((blk, 1), F32), jnp.zeros((blk, LANES), F32))
            m, l, acc = lax.fori_loop(0, qi, lambda ki, st: kv_step(ki, *st, False), init)
            m, l, acc = kv_step(qi, m, l, acc, True)
            outs.append(acc / l)
        o_ref[pl.ds(q0, blk), :] = jnp.where(lane < FOX_HD, outs[0], outs[1]).astype(o_ref.dtype)
        return carry

    lax.fori_loop(0, seq // blk, q_body, 0)


def _fox_call(fox, c, ct4, batch, seq, blk):
    pairs = FOX_WIDTH // LANES
    return pl.pallas_call(
        functools.partial(_fox_kernel, blk=blk),
        grid=(batch, pairs),
        in_specs=[
            pl.BlockSpec((seq, LANES), lambda b, p: (b, p)),
            pl.BlockSpec((seq, LANES), lambda b, p: (b, pairs + p)),
            pl.BlockSpec((seq, LANES), lambda b, p: (b, 2 * pairs + p)),
            pl.BlockSpec((seq, LANES), lambda b, p: (b, 0)),
            pl.BlockSpec((1, FOX_HEADS, seq // blk, blk), lambda b, p: (b, 0, 0, 0)),
        ],
        out_specs=pl.BlockSpec((seq, LANES), lambda b, p: (b, p)),
        out_shape=jax.ShapeDtypeStruct((batch * seq, FOX_WIDTH), BF16),
        compiler_params=pltpu.CompilerParams(
            dimension_semantics=("arbitrary", "arbitrary"), vmem_limit_bytes=VMEM_LIMIT),
        name="fox",
    )(fox, fox, fox, c, ct4)


def _gelu_tanh(a):
    return 0.5 * a * (1.0 + jnp.tanh(0.7978845608028654 * (a + 0.044715 * (a * a * a))))


def _ffn_kernel(x_ref, og_ref, of_ref, gate_ref, wg_ref, wf_ref, wo_ref, n2_ref,
                wa_ref, wv_ref, cw_ref, cb_ref, wd_ref, nf_ref, out_ref,
                carry_ref, stage_ref, acc_ref, h2_ref):
    tm = x_ref.shape[0]

    @pl.when(pl.program_id(1) == 0)
    def _():
        carry_ref[...] = jnp.zeros_like(carry_ref)

    y = (gate_ref[:, 0:D_MODEL].astype(F32) * _dot(og_ref[...], wg_ref[...])
         + gate_ref[:, D_MODEL:].astype(F32) * _dot(of_ref[...], wf_ref[...]))
    x1 = x_ref[...] + _dot(y.astype(BF16), wo_ref[...])
    acc_ref[...] = x1
    h2_ref[...] = _rms(x1, n2_ref[...]).astype(BF16)

    def ff_step(j, carry):
        h2 = h2_ref[...]
        ua = _dot(h2, wa_ref[j])
        uv = _dot(h2, wv_ref[j])
        stage_ref[0:SUBLANES, :] = carry_ref[j]
        stage_ref[SUBLANES:, :] = ua
        carry_ref[j] = ua[tm - SUBLANES:, :]
        cw = cw_ref[j]
        a = (cb_ref[j] + cw[0:1, :] * stage_ref[SUBLANES - 2:SUBLANES - 2 + tm, :]
             + cw[1:2, :] * stage_ref[SUBLANES - 1:SUBLANES - 1 + tm, :] + cw[2:3, :] * ua)
        act = (_gelu_tanh(a) * uv).astype(BF16)
        acc_ref[...] += _dot(act, wd_ref[j])
        return carry

    lax.fori_loop(0, N_FF_CHUNKS, ff_step, 0)
    out_ref[...] = _rms(acc_ref[...], nf_ref[...])


def _ffn_call(x2, og, of, gate, wg, wf, wo, n2, wa, wv, cw, cb, wd, nf, batch, seq, tm):
    n = seq // tm
    rows = lambda b, i: (b * n + i, 0)

    def resident(shape):
        return pl.BlockSpec(shape, lambda b, i: (0,) * len(shape), pipeline_mode=pl.Buffered(1))

    return pl.pallas_call(
        _ffn_kernel,
        grid=(batch, n),
        in_specs=[
            pl.BlockSpec((tm, D_MODEL), rows),
            pl.BlockSpec((tm, GLA_WIDTH), rows),
            pl.BlockSpec((tm, FOX_WIDTH), rows),
            pl.BlockSpec((tm, 2 * D_MODEL), rows),
            resident((GLA_WIDTH, D_MODEL)),
            resident((FOX_WIDTH, D_MODEL)),
            resident((D_MODEL, D_MODEL)),
            resident((1, D_MODEL)),
            resident((N_FF_CHUNKS, D_MODEL, FF_CHUNK)),
            resident((N_FF_CHUNKS, D_MODEL, FF_CHUNK)),
            resident((N_FF_CHUNKS, SUBLANES, FF_CHUNK)),
            resident((N_FF_CHUNKS, 1, FF_CHUNK)),
            resident((N_FF_CHUNKS, FF_CHUNK, D_MODEL)),
            resident((1, D_MODEL)),
        ],
        out_specs=pl.BlockSpec((tm, D_MODEL), rows),
        out_shape=jax.ShapeDtypeStruct((batch * seq, D_MODEL), F32),
        scratch_shapes=[
            pltpu.VMEM((N_FF_CHUNKS, SUBLANES, FF_CHUNK), F32),
            pltpu.VMEM((tm + SUBLANES, FF_CHUNK), F32),
            pltpu.VMEM((tm, D_MODEL), F32),
            pltpu.VMEM((tm, D_MODEL), BF16),
        ],
        compiler_params=pltpu.CompilerParams(
            dimension_semantics=("arbitrary", "arbitrary"), vmem_limit_bytes=VMEM_LIMIT),
        name="ffn",
    )(x2, og, of, gate, wg, wf, wo, n2, wa, wv, cw, cb, wd, nf)


def _pad_cols(w, width):
    return jnp.pad(w, ((0, 0), (0, width - w.shape[1])))


def _layer(x2, batch, seq, norm_mix_g, w_in, gla_alpha_w2, gla_alpha_b, gla_out_norm_g,
           fox_forget_b, fox_q_norm_g, fox_k_norm_g, gate_b, w_gla_branch, w_fox_branch,
           w_out, norm_ffn_g, w_up, conv_w, conv_b, w_down, final_g):
    offs = [0]
    for s in IN_SPLITS:
        offs.append(offs[-1] + s)
    sec = lambda a, b: w_in[:, offs[a]:offs[b]]
    w_packed = jnp.concatenate([
        sec(0, 4), _pad_cols(sec(4, 5), LANES), sec(5, 8), _pad_cols(sec(8, 9), LANES), sec(9, 11),
    ], axis=1).astype(BF16)
    a2 = jnp.pad(gla_alpha_w2, ((0, LANES - GLA_LOWRANK), (0, 0)))
    fb = jnp.pad(fox_forget_b, (0, LANES - FOX_HEADS)).reshape(1, LANES)
    qg = (jnp.tile(fox_q_norm_g, FOX_HEADS) * (FOX_HD ** -0.5)).reshape(1, FOX_WIDTH)
    kg = jnp.tile(fox_k_norm_g, FOX_HEADS).reshape(1, FOX_WIDTH)
    head_of = jnp.arange(FOX_WIDTH) // FOX_HD
    bd = (head_of[:, None] == head_of[None, :]).astype(BF16)

    tm = min(512, seq)
    gla_in, la, fox, c, ct, gates = _proj_call(
        x2, norm_mix_g.reshape(1, D_MODEL), w_packed, a2, gla_alpha_b.reshape(1, GLA_KW), fb, qg, kg,
        gate_b.reshape(1, 2 * D_MODEL), bd, batch, seq, tm)

    o_gla = _gla_call(gla_in, la, gla_out_norm_g.reshape(1, GLA_WIDTH), batch, seq, min(256, seq))

    blk = min(256, seq)
    o_fox = _fox_call(fox, c, ct.reshape(batch, FOX_HEADS, seq // blk, blk), batch, seq, blk)

    wa = w_up[:, :D_FF].reshape(D_MODEL, N_FF_CHUNKS, FF_CHUNK).transpose(1, 0, 2).astype(BF16)
    wv = w_up[:, D_FF:].reshape(D_MODEL, N_FF_CHUNKS, FF_CHUNK).transpose(1, 0, 2).astype(BF16)
    wd = w_down.reshape(N_FF_CHUNKS, FF_CHUNK, D_MODEL).astype(BF16)
    cw = jnp.pad(conv_w, ((0, SUBLANES - CONV_W), (0, 0))).reshape(SUBLANES, N_FF_CHUNKS, FF_CHUNK).transpose(1, 0, 2)
    cb = conv_b.reshape(N_FF_CHUNKS, 1, FF_CHUNK)
    return _ffn_call(
        x2, o_gla, o_fox, gates, w_gla_branch.astype(BF16), w_fox_branch.astype(BF16), w_out.astype(BF16),
        norm_ffn_g.reshape(1, D_MODEL), wa, wv, cw, cb, wd, final_g.reshape(1, D_MODEL), batch, seq, tm)


def kernel(x, norm_mix_g, w_in, gla_alpha_w2, gla_alpha_b, gla_out_norm_g, fox_forget_b, fox_q_norm_g,
           fox_k_norm_g, gate_b, w_gla_branch, w_fox_branch, w_out, norm_ffn_g, w_up, conv_w, conv_b,
           w_down, norm_final_g):
    batch, seq, _ = x.shape
    depth = norm_mix_g.shape[0]
    assert depth == 1, "the final RMSNorm is fused into the (single) layer's last kernel"
    out = _layer(x.reshape(batch * seq, D_MODEL), batch, seq, norm_mix_g[0], w_in[0], gla_alpha_w2[0],
                 gla_alpha_b[0], gla_out_norm_g[0], fox_forget_b[0], fox_q_norm_g[0], fox_k_norm_g[0],
                 gate_b[0], w_gla_branch[0], w_fox_branch[0], w_out[0], norm_ffn_g[0], w_up[0],
                 conv_w[0], conv_b[0], w_down[0], norm_final_g)
    return out.reshape(batch, seq, D_MODEL)
```

```python
import functools

import jax
import jax.numpy as jnp
from jax import lax
from jax.experimental import pallas as pl
from jax.experimental.pallas import tpu as pltpu

D_MODEL = 1024
CHUNK = 64
GLA_HEADS = 4
GLA_DK = 64
GLA_DV = 128
GLA_LOWRANK = 16
GLA_TAU = 16.0
FOX_HEADS = 8
FOX_HD = 64
D_FF = 2816
CONV_W = 3
EPS = 1e-6

GLA_KW = GLA_HEADS * GLA_DK
GLA_WIDTH = GLA_HEADS * GLA_DV
FOX_WIDTH = FOX_HEADS * FOX_HD
IN_SPLITS = (GLA_KW, GLA_KW, GLA_WIDTH, GLA_WIDTH, GLA_LOWRANK,
             FOX_WIDTH, FOX_WIDTH, FOX_WIDTH, FOX_HEADS, D_MODEL, D_MODEL)

LANES = 128
SUBLANES = 8
MXU_N = 256
VMEM_LIMIT = 56 * 1024 * 1024

SEC_GLA = 2 * GLA_KW + 2 * GLA_WIDTH
FOX_SPREAD = FOX_HEADS * LANES
OFF_GLA = 0
OFF_LR = OFF_GLA + SEC_GLA
OFF_FQ = OFF_LR + LANES
OFF_FK = OFF_FQ + FOX_SPREAD
OFF_FV = OFF_FK + FOX_SPREAD
OFF_FF = OFF_FV + FOX_WIDTH
OFF_GATE = OFF_FF + LANES
PACKED_COLS = OFF_GATE + 2 * D_MODEL
N_SPLIT = 3
LOG2E = 1.4426950408889634

ROW_TILE = 512
GLA_TILE = 256
FF_CHUNK = MXU_N
N_FF_CHUNKS = D_FF // FF_CHUNK

BF16 = jnp.bfloat16
F32 = jnp.float32
NEG = -1e30


def _dot(a, b):
    return jnp.dot(a, b, preferred_element_type=F32)


def _dot_nt(a, b):
    return lax.dot_general(a, b, (((1,), (1,)), ((), ())), preferred_element_type=F32)


def _dot_tn(a, b):
    return lax.dot_general(a, b, (((0,), (0,)), ((), ())), preferred_element_type=F32)


def _rms(x, g):
    return x * lax.rsqrt(jnp.mean(x * x, axis=-1, keepdims=True) + EPS) * g


def _log_sigmoid(z):
    return jnp.minimum(z, 0.0) - jnp.log1p(jnp.exp(-jnp.abs(z)))


def _split3(a):
    hi = a.astype(BF16)
    r1 = a - hi.astype(F32)
    mid = r1.astype(BF16)
    lo = (r1 - mid.astype(F32)).astype(BF16)
    return hi, mid, lo


def _cumsum_rows(tri, a):
    hi, mid, lo = _split3(a)
    return _dot(tri, hi) + _dot(tri, mid) + _dot(tri, lo)


def _lower_tri(n):
    r = lax.broadcasted_iota(jnp.int32, (n, n), 0)
    c = lax.broadcasted_iota(jnp.int32, (n, n), 1)
    return r >= c


def _proj_kernel(x_ref, g_ref, w_ref, a2_ref, ab_ref, fb_ref, qg_ref, kg_ref, gb_ref, place_ref,
                 gla_ref, la_ref, fq_ref, fk_ref, vt_ref, ct_ref, gate_ref, carry_ref):
    tm = x_ref.shape[0]

    @pl.when(pl.program_id(1) == 0)
    def _():
        carry_ref[...] = jnp.zeros_like(carry_ref)

    h = _rms(x_ref[...], g_ref[...]).astype(BF16)

    gla_ref[...] = _dot(h, w_ref[:, OFF_GLA:OFF_GLA + SEC_GLA]).astype(BF16)

    glr = _dot(h, w_ref[:, OFF_LR:OFF_LR + LANES])
    z = jnp.dot(glr, a2_ref[...], preferred_element_type=F32,
                precision=lax.Precision.HIGHEST) + ab_ref[...]
    la_ref[...] = _log_sigmoid(z) * (1.0 / GLA_TAU)

    lf = _log_sigmoid(_dot(h, w_ref[:, OFF_FF:OFF_FF + LANES]) + fb_ref[...])
    c = _cumsum_rows(_lower_tri(tm).astype(BF16), lf) + carry_ref[...]
    carry_ref[...] = c[tm - 1:tm, :]
    ct_ref[0] = jnp.transpose(c)[0:FOX_HEADS, :]
    c_terms = _split3(c * (-LOG2E))
    c_place = sum(_dot(c_terms[t], place_ref[t]) for t in range(N_SPLIT))

    lane = lax.broadcasted_iota(jnp.int32, (1, LANES), 1)
    ones = jnp.where((lane >= FOX_HD) & (lane < FOX_HD + N_SPLIT), 1.0, 0.0)

    def head_norm(t, g):
        ms = jnp.sum(t * t, axis=-1, keepdims=True) * (1.0 / FOX_HD)
        return t * lax.rsqrt(ms + EPS) * g

    fq = _dot(h, w_ref[:, OFF_FQ:OFF_FQ + FOX_SPREAD])
    fk = _dot(h, w_ref[:, OFF_FK:OFF_FK + FOX_SPREAD])
    for hd in range(FOX_HEADS):
        sl = slice(hd * LANES, (hd + 1) * LANES)
        fq_ref[:, sl] = (head_norm(fq[:, sl], qg_ref[:, sl]) + ones).astype(BF16)
        fk_ref[:, sl] = (head_norm(fk[:, sl], kg_ref[:, sl]) + c_place[:, sl]).astype(BF16)

    fv = _dot(h, w_ref[:, OFF_FV:OFF_FV + FOX_WIDTH])
    vt_ref[0, 0] = jnp.transpose(fv).astype(BF16)

    gate_ref[...] = jax.nn.sigmoid(_dot(h, w_ref[:, OFF_GATE:]) + gb_ref[...]).astype(BF16)


def _resident(shape):
    return pl.BlockSpec(shape, lambda b, i: (0,) * len(shape), pipeline_mode=pl.Buffered(1))


def _proj_call(x2, g, w, a2, ab, fb, qg, kg, gb, place, batch, seq):
    tm = ROW_TILE
    n = seq // tm
    rows = lambda b, i: (b * n + i, 0)
    tokens = batch * seq
    return pl.pallas_call(
        _proj_kernel,
        grid=(batch, n),
        in_specs=[
            pl.BlockSpec((tm, D_MODEL), rows),
            _resident((1, D_MODEL)),
            _resident((D_MODEL, PACKED_COLS)),
            _resident((LANES, GLA_KW)),
            _resident((1, GLA_KW)),
            _resident((1, LANES)),
            _resident((1, FOX_SPREAD)),
            _resident((1, FOX_SPREAD)),
            _resident((1, 2 * D_MODEL)),
            _resident((N_SPLIT, LANES, FOX_SPREAD)),
        ],
        out_specs=[
            pl.BlockSpec((tm, SEC_GLA), rows),
            pl.BlockSpec((tm, GLA_KW), rows),
            pl.BlockSpec((tm, FOX_SPREAD), rows),
            pl.BlockSpec((tm, FOX_SPREAD), rows),
            pl.BlockSpec((1, 1, FOX_WIDTH, tm), lambda b, i: (b, i, 0, 0)),
            pl.BlockSpec((1, FOX_HEADS, tm), lambda b, i: (b, 0, i)),
            pl.BlockSpec((tm, 2 * D_MODEL), rows),
        ],
        out_shape=[
            jax.ShapeDtypeStruct((tokens, SEC_GLA), BF16),
            jax.ShapeDtypeStruct((tokens, GLA_KW), F32),
            jax.ShapeDtypeStruct((tokens, FOX_SPREAD), BF16),
            jax.ShapeDtypeStruct((tokens, FOX_SPREAD), BF16),
            jax.ShapeDtypeStruct((batch, n, FOX_WIDTH, tm), BF16),
            jax.ShapeDtypeStruct((batch, FOX_HEADS, seq), F32),
            jax.ShapeDtypeStruct((tokens, 2 * D_MODEL), BF16),
        ],
        scratch_shapes=[pltpu.VMEM((1, LANES), F32)],
        compiler_params=pltpu.CompilerParams(
            dimension_semantics=("arbitrary", "arbitrary"), vmem_limit_bytes=VMEM_LIMIT),
        name="proj",
    )(x2, g, w, a2, ab, fb, qg, kg, gb, place)


def _gla_kernel(gin_ref, la_ref, ng_ref, o_ref, st_ref):
    tm = gin_ref.shape[0]

    @pl.when(pl.program_id(1) == 0)
    def _():
        st_ref[...] = jnp.zeros_like(st_ref)

    causal = _lower_tri(CHUNK)
    tri = causal.astype(BF16)
    lane_head = lax.broadcasted_iota(jnp.int32, (1, GLA_KW), 1) // GLA_DK
    st_r = lax.broadcasted_iota(jnp.int32, (GLA_KW, GLA_WIDTH), 0) // GLA_DK
    st_c = lax.broadcasted_iota(jnp.int32, (GLA_KW, GLA_WIDTH), 1) // GLA_DV
    on_diag = st_r == st_c
    ng = ng_ref[...]

    for ci in range(tm // CHUNK):
        rows = slice(ci * CHUNK, (ci + 1) * CHUNK)
        b = _cumsum_rows(tri, la_ref[rows, :])
        b_last = b[CHUNK - 1:CHUNK, :]
        q = gin_ref[rows, 0:GLA_KW].astype(F32) * (GLA_DK ** -0.5)
        k = gin_ref[rows, GLA_KW:2 * GLA_KW].astype(F32)
        v = gin_ref[rows, 2 * GLA_KW:2 * GLA_KW + GLA_WIDTH]
        r = gin_ref[rows, 2 * GLA_KW + GLA_WIDTH:].astype(F32)
        q_e = (q * jnp.exp(b)).astype(BF16)
        k_e = (k * jnp.exp(-b)).astype(BF16)
        k_d = (k * jnp.exp(b_last - b)).astype(BF16)

        state = st_ref[...]
        o_inter = _dot(q_e, state.astype(BF16))
        parts = []
        for hd in range(GLA_HEADS):
            q_h = jnp.where(lane_head == hd, q_e, jnp.zeros_like(q_e))
            s = jnp.where(causal, _dot_nt(q_h, k_e), 0.0).astype(BF16)
            parts.append(_dot(s, v[:, hd * GLA_DV:(hd + 1) * GLA_DV]))
        o = jnp.concatenate(parts, axis=1) + o_inter

        kv = _dot_tn(k_d, v)
        dec = jnp.transpose(jnp.broadcast_to(jnp.exp(b_last), (LANES, GLA_KW)))
        dec = jnp.concatenate([dec] * (GLA_WIDTH // LANES), axis=1)
        st_ref[...] = dec * state + jnp.where(on_diag, kv, 0.0)

        outs = []
        for hd in range(GLA_HEADS):
            sl = slice(hd * GLA_DV, (hd + 1) * GLA_DV)
            outs.append(_rms(o[:, sl], ng[:, sl]))
        o_ref[rows, :] = (jnp.concatenate(outs, axis=1) * (r * jax.nn.sigmoid(r))).astype(o_ref.dtype)


def _gla_call(gin, la, ng, batch, seq):
    tm = GLA_TILE
    n = seq // tm
    rows = lambda b, i: (b * n + i, 0)
    return pl.pallas_call(
        _gla_kernel,
        grid=(batch, n),
        in_specs=[
            pl.BlockSpec((tm, SEC_GLA), rows),
            pl.BlockSpec((tm, GLA_KW), rows),
            pl.BlockSpec((1, GLA_WIDTH), lambda b, i: (0, 0)),
        ],
        out_specs=pl.BlockSpec((tm, GLA_WIDTH), rows),
        out_shape=jax.ShapeDtypeStruct((batch * seq, GLA_WIDTH), BF16),
        scratch_shapes=[pltpu.VMEM((GLA_KW, GLA_WIDTH), F32)],
        compiler_params=pltpu.CompilerParams(
            dimension_semantics=("arbitrary", "arbitrary"), vmem_limit_bytes=VMEM_LIMIT),
        name="gla",
    )(gin, la, ng)


def _fox_kernel(q_ref, k_ref, vt_ref, ct_ref, o_ref):
    hp = pl.program_id(1)
    seq = q_ref.shape[0]
    blk = vt_ref.shape[3]
    r_i = lax.broadcasted_iota(jnp.int32, (blk, blk), 0)
    c_i = lax.broadcasted_iota(jnp.int32, (blk, blk), 1)
    visible = r_i <= c_i

    def q_body(qi, carry):
        q0 = pl.multiple_of(qi * blk, blk)
        qs = [q_ref[pl.ds(q0, blk), j * LANES:(j + 1) * LANES] for j in range(2)]
        cqs = [ct_ref[0, 2 * hp + j, pl.ds(qi, 1), :] * LOG2E for j in range(2)]

        def kv_step(ki, state, masked):
            k0 = pl.multiple_of(ki * blk, blk)
            new = []
            for j in range(2):
                m, l, acc = state[j]
                s = _dot_nt(k_ref[pl.ds(k0, blk), j * LANES:(j + 1) * LANES], qs[j])
                if masked:
                    s = jnp.where(visible, s, NEG)
                m_new = jnp.maximum(m, jnp.max(s, axis=0, keepdims=True) + cqs[j])
                p = jnp.exp2(s - (m_new - cqs[j]))
                alpha = jnp.exp2(m - m_new)
                l = alpha * l + jnp.sum(p, axis=0, keepdims=True)
                acc = alpha * acc + _dot(vt_ref[0, ki, j * FOX_HD:(j + 1) * FOX_HD, :], p.astype(BF16))
                new.append((m_new, l, acc))
            return tuple(new)

        init = tuple((jnp.full((1, blk), NEG, F32), jnp.zeros((1, blk), F32), jnp.zeros((FOX_HD, blk), F32))
                     for _ in range(2))
        state = lax.fori_loop(0, qi, lambda ki, st: kv_step(ki, st, False), init)
        state = kv_step(qi, state, True)
        o_t = jnp.concatenate([acc / l for (_, l, acc) in state], axis=0)
        o_ref[pl.ds(q0, blk), :] = jnp.transpose(o_t).astype(o_ref.dtype)
        return carry

    lax.fori_loop(0, seq // blk, q_body, 0)


def _fox_call(fq, fk, vt, ct4, batch, seq):
    blk = ROW_TILE
    pairs = FOX_WIDTH // LANES
    return pl.pallas_call(
        _fox_kernel,
        grid=(batch, pairs),
        in_specs=[
            pl.BlockSpec((seq, 2 * LANES), lambda b, p: (b, p)),
            pl.BlockSpec((seq, 2 * LANES), lambda b, p: (b, p)),
            pl.BlockSpec((1, seq // blk, LANES, blk), lambda b, p: (b, 0, p, 0)),
            pl.BlockSpec((1, FOX_HEADS, seq // blk, blk), lambda b, p: (b, 0, 0, 0)),
        ],
        out_specs=pl.BlockSpec((seq, LANES), lambda b, p: (b, p)),
        out_shape=jax.ShapeDtypeStruct((batch * seq, FOX_WIDTH), BF16),
        compiler_params=pltpu.CompilerParams(
            dimension_semantics=("arbitrary", "arbitrary"), vmem_limit_bytes=VMEM_LIMIT),
        name="fox",
    )(fq, fk, vt, ct4)


def _gelu_tanh(a):
    return 0.5 * a * (1.0 + jnp.tanh(0.7978845608028654 * (a + 0.044715 * (a * a * a))))


def _ffn_kernel(x_ref, og_ref, of_ref, gate_ref, wg_ref, wf_ref, wo_ref, n2_ref,
                wa_ref, wv_ref, cw_ref, cb_ref, wd_ref, nf_ref, out_ref,
                carry_ref, stage_ref, acc_ref, h2_ref):
    tm = x_ref.shape[0]

    @pl.when(pl.program_id(1) == 0)
    def _():
        carry_ref[...] = jnp.zeros_like(carry_ref)

    y = (gate_ref[:, 0:D_MODEL].astype(F32) * _dot(og_ref[...], wg_ref[...])
         + gate_ref[:, D_MODEL:].astype(F32) * _dot(of_ref[...], wf_ref[...]))
    x1 = x_ref[...] + _dot(y.astype(BF16), wo_ref[...])
    acc_ref[...] = x1
    h2_ref[...] = _rms(x1, n2_ref[...]).astype(BF16)

    def ff_step(j, carry):
        h2 = h2_ref[...]
        ua = _dot(h2, wa_ref[j])
        uv = _dot(h2, wv_ref[j])
        stage_ref[0:SUBLANES, :] = carry_ref[j]
        stage_ref[SUBLANES:, :] = ua
        carry_ref[j] = ua[tm - SUBLANES:, :]
        cw = cw_ref[j]
        a = (cb_ref[j] + cw[0:1, :] * stage_ref[SUBLANES - 2:SUBLANES - 2 + tm, :]
             + cw[1:2, :] * stage_ref[SUBLANES - 1:SUBLANES - 1 + tm, :] + cw[2:3, :] * ua)
        act = (_gelu_tanh(a) * uv).astype(BF16)
        acc_ref[...] += _dot(act, wd_ref[j])
        return carry

    lax.fori_loop(0, N_FF_CHUNKS, ff_step, 0)
    out_ref[...] = _rms(acc_ref[...], nf_ref[...])


def _ffn_call(x2, og, of, gate, wg, wf, wo, n2, wa, wv, cw, cb, wd, nf, batch, seq):
    tm = ROW_TILE
    n = seq // tm
    rows = lambda b, i: (b * n + i, 0)
    return pl.pallas_call(
        _ffn_kernel,
        grid=(batch, n),
        in_specs=[
            pl.BlockSpec((tm, D_MODEL), rows),
            pl.BlockSpec((tm, GLA_WIDTH), rows),
            pl.BlockSpec((tm, FOX_WIDTH), rows),
            pl.BlockSpec((tm, 2 * D_MODEL), rows),
            _resident((GLA_WIDTH, D_MODEL)),
            _resident((FOX_WIDTH, D_MODEL)),
            _resident((D_MODEL, D_MODEL)),
            _resident((1, D_MODEL)),
            _resident((N_FF_CHUNKS, D_MODEL, FF_CHUNK)),
            _resident((N_FF_CHUNKS, D_MODEL, FF_CHUNK)),
            _resident((N_FF_CHUNKS, SUBLANES, FF_CHUNK)),
            _resident((N_FF_CHUNKS, 1, FF_CHUNK)),
            _resident((N_FF_CHUNKS, FF_CHUNK, D_MODEL)),
            _resident((1, D_MODEL)),
        ],
        out_specs=pl.BlockSpec((tm, D_MODEL), rows),
        out_shape=jax.ShapeDtypeStruct((batch * seq, D_MODEL), F32),
        scratch_shapes=[
            pltpu.VMEM((N_FF_CHUNKS, SUBLANES, FF_CHUNK), F32),
            pltpu.VMEM((tm + SUBLANES, FF_CHUNK), F32),
            pltpu.VMEM((tm, D_MODEL), F32),
            pltpu.VMEM((tm, D_MODEL), BF16),
        ],
        compiler_params=pltpu.CompilerParams(
            dimension_semantics=("arbitrary", "arbitrary"), vmem_limit_bytes=VMEM_LIMIT),
        name="ffn",
    )(x2, og, of, gate, wg, wf, wo, n2, wa, wv, cw, cb, wd, nf)


def _pad_cols(w, width):
    return jnp.pad(w, ((0, 0), (0, width - w.shape[1])))


def _spread_heads(w):
    lead = w.shape[:-1]
    w = w.reshape(lead + (FOX_HEADS, FOX_HD))
    w = jnp.pad(w, [(0, 0)] * len(lead) + [(0, 0), (0, LANES - FOX_HD)])
    return w.reshape(lead + (FOX_SPREAD,))


def _layer(x2, batch, seq, norm_mix_g, w_in, gla_alpha_w2, gla_alpha_b, gla_out_norm_g,
           fox_forget_b, fox_q_norm_g, fox_k_norm_g, gate_b, w_gla_branch, w_fox_branch,
           w_out, norm_ffn_g, w_up, conv_w, conv_b, w_down, final_g):
    offs = [0]
    for s in IN_SPLITS:
        offs.append(offs[-1] + s)
    sec = lambda a, b: w_in[:, offs[a]:offs[b]]
    w_packed = jnp.concatenate([
        sec(0, 4), _pad_cols(sec(4, 5), LANES), _spread_heads(sec(5, 6)), _spread_heads(sec(6, 7)),
        sec(7, 8), _pad_cols(sec(8, 9), LANES), sec(9, 11),
    ], axis=1).astype(BF16)
    a2 = jnp.pad(gla_alpha_w2, ((0, LANES - GLA_LOWRANK), (0, 0)))
    fb = jnp.pad(fox_forget_b, (0, LANES - FOX_HEADS)).reshape(1, LANES)
    qg = _spread_heads(jnp.tile(fox_q_norm_g, FOX_HEADS) * (FOX_HD ** -0.5 * LOG2E)).reshape(1, FOX_SPREAD)
    kg = _spread_heads(jnp.tile(fox_k_norm_g, FOX_HEADS)).reshape(1, FOX_SPREAD)
    t_i = jnp.arange(N_SPLIT)[:, None, None]
    h_i = jnp.arange(LANES)[None, :, None]
    col = jnp.arange(FOX_SPREAD)[None, None, :]
    place = ((h_i < FOX_HEADS) & (col == h_i * LANES + FOX_HD + t_i)).astype(BF16)

    gla_in, la, fq, fk, vt, ct, gates = _proj_call(
        x2, norm_mix_g.reshape(1, D_MODEL), w_packed, a2, gla_alpha_b.reshape(1, GLA_KW), fb, qg, kg,
        gate_b.reshape(1, 2 * D_MODEL), place, batch, seq)

    o_gla = _gla_call(gla_in, la, gla_out_norm_g.reshape(1, GLA_WIDTH), batch, seq)

    o_fox = _fox_call(fq, fk, vt, ct.reshape(batch, FOX_HEADS, seq // ROW_TILE, ROW_TILE), batch, seq)

    wa = w_up[:, :D_FF].reshape(D_MODEL, N_FF_CHUNKS, FF_CHUNK).transpose(1, 0, 2).astype(BF16)
    wv = w_up[:, D_FF:].reshape(D_MODEL, N_FF_CHUNKS, FF_CHUNK).transpose(1, 0, 2).astype(BF16)
    wd = w_down.reshape(N_FF_CHUNKS, FF_CHUNK, D_MODEL).astype(BF16)
    cw = jnp.pad(conv_w, ((0, SUBLANES - CONV_W), (0, 0))).reshape(SUBLANES, N_FF_CHUNKS, FF_CHUNK).transpose(1, 0, 2)
    cb = conv_b.reshape(N_FF_CHUNKS, 1, FF_CHUNK)
    return _ffn_call(
        x2, o_gla, o_fox, gates, w_gla_branch.astype(BF16), w_fox_branch.astype(BF16), w_out.astype(BF16),
        norm_ffn_g.reshape(1, D_MODEL), wa, wv, cw, cb, wd, final_g.reshape(1, D_MODEL), batch, seq)


def kernel(x, norm_mix_g, w_in, gla_alpha_w2, gla_alpha_b, gla_out_norm_g, fox_forget_b, fox_q_norm_g,
           fox_k_norm_g, gate_b, w_gla_branch, w_fox_branch, w_out, norm_ffn_g, w_up, conv_w, conv_b,
           w_down, norm_final_g):
    batch, seq, _ = x.shape
    assert norm_mix_g.shape[0] == 1, "the final RMSNorm is fused into the single layer's last kernel"
    assert seq % ROW_TILE == 0
    out = _layer(x.reshape(batch * seq, D_MODEL), batch, seq, norm_mix_g[0], w_in[0], gla_alpha_w2[0],
                 gla_alpha_b[0], gla_out_norm_g[0], fox_forget_b[0], fox_q_norm_g[0], fox_k_norm_g[0],
                 gate_b[0], w_gla_branch[0], w_fox_branch[0], w_out[0], norm_ffn_g[0], w_up[0],
                 conv_w[0], conv_b[0], w_down[0], norm_final_g)
    return out.reshape(batch, seq, D_MODEL)
```

```python
import jax
import jax.numpy as jnp
from jax import lax
from jax.experimental import pallas as pl
from jax.experimental.pallas import tpu as pltpu

D_MODEL = 1024
CHUNK = 64
GLA_HEADS = 4
GLA_DK = 64
GLA_DV = 128
GLA_LOWRANK = 16
GLA_TAU = 16.0
FOX_HEADS = 8
FOX_HD = 64
D_FF = 2816
CONV_W = 3
EPS = 1e-6

GLA_KW = GLA_HEADS * GLA_DK
GLA_WIDTH = GLA_HEADS * GLA_DV
FOX_WIDTH = FOX_HEADS * FOX_HD
IN_SPLITS = (GLA_KW, GLA_KW, GLA_WIDTH, GLA_WIDTH, GLA_LOWRANK,
             FOX_WIDTH, FOX_WIDTH, FOX_WIDTH, FOX_HEADS, D_MODEL, D_MODEL)

LANES = 128
SUBLANES = 8
BF16_ROWS = 16
MXU_N = 256
VMEM_LIMIT = 56 * 1024 * 1024

N_SPLIT = 3
LOG2E = 1.4426950408889634

SEC_GLA = 2 * GLA_KW + 2 * GLA_WIDTH
FOX_SPREAD = FOX_HEADS * LANES
LR_COPIES = 6
OFF_FF_IN_SMALL = LR_COPIES * GLA_LOWRANK
OFF_GLA = 0
OFF_SMALL = OFF_GLA + SEC_GLA
OFF_FQ = OFF_SMALL + LANES
OFF_FK = OFF_FQ + FOX_SPREAD
OFF_FV = OFF_FK + FOX_SPREAD
OFF_GATE = OFF_FV + FOX_WIDTH
PACKED_COLS = OFF_GATE + 2 * D_MODEL

ROW_TILE = 512
FF_CHUNK = MXU_N
N_FF_CHUNKS = D_FF // FF_CHUNK
FOX_GROUP = 4
VT_ROWS = FOX_HD + BF16_ROWS

BF16 = jnp.bfloat16
F32 = jnp.float32
NEG = -1e30


def _dot(a, b):
    return jnp.dot(a, b, preferred_element_type=F32)


def _dot_nt(a, b):
    return lax.dot_general(a, b, (((1,), (1,)), ((), ())), preferred_element_type=F32)


def _dot_tn(a, b):
    return lax.dot_general(a, b, (((0,), (0,)), ((), ())), preferred_element_type=F32)


def _rms(x, g):
    return x * lax.rsqrt(jnp.mean(x * x, axis=-1, keepdims=True) + EPS) * g


def _log_sigmoid(z):
    return jnp.minimum(z, 0.0) - jnp.log1p(jnp.exp(-jnp.abs(z)))


def _split3(a):
    hi = a.astype(BF16)
    r1 = a - hi.astype(F32)
    mid = r1.astype(BF16)
    lo = (r1 - mid.astype(F32)).astype(BF16)
    return hi, mid, lo


def _cumsum_rows_mxu(tri, a):
    hi, mid, lo = _split3(a)
    return _dot(tri, hi) + _dot(tri, mid) + _dot(tri, lo)


def _cumsum_rows_scan(x):
    n, w = x.shape
    row = lax.broadcasted_iota(jnp.int32, (n, w), 0)
    d = 1
    while d < n:
        if d < SUBLANES:
            shifted = jnp.where(row >= d, pltpu.roll(x, d, axis=0), 0.0)
        else:
            shifted = jnp.concatenate([jnp.zeros((d, w), F32), x[:n - d]], axis=0)
        x = x + shifted
        d *= 2
    return x


def _lower_tri(n):
    r = lax.broadcasted_iota(jnp.int32, (n, n), 0)
    c = lax.broadcasted_iota(jnp.int32, (n, n), 1)
    return r >= c


def _resident(shape):
    return pl.BlockSpec(shape, lambda b, i: (0,) * len(shape), pipeline_mode=pl.Buffered(1))


_GRID_PARAMS = pltpu.CompilerParams(
    dimension_semantics=("arbitrary", "arbitrary"), vmem_limit_bytes=VMEM_LIMIT)


def _proj_kernel(x_ref, g_ref, w_ref, a2_ref, ab_ref, fb_ref, qg_ref, kg_ref, gb_ref, place_ref,
                 gla_ref, la_ref, fq_ref, fk_ref, vt_ref, ct_ref, gate_ref, carry_ref):
    tm = x_ref.shape[0]

    @pl.when(pl.program_id(1) == 0)
    def _():
        carry_ref[...] = jnp.zeros_like(carry_ref)

    h = _rms(x_ref[...], g_ref[...]).astype(BF16)

    gla_ref[...] = _dot(h, w_ref[:, OFF_GLA:OFF_GLA + SEC_GLA]).astype(BF16)

    small = _dot(h, w_ref[:, OFF_SMALL:OFF_SMALL + LANES])
    lane = lax.broadcasted_iota(jnp.int32, (1, LANES), 1)

    s_hi, s_mid, s_lo = _split3(small)
    grp = lane // GLA_LOWRANK
    glr_terms = jnp.where((grp == 2) | (grp == 4), s_mid, jnp.where(grp == 5, s_lo, s_hi))
    z = _dot(glr_terms, a2_ref[...]) + ab_ref[...]
    la_ref[...] = _log_sigmoid(z) * (1.0 / GLA_TAU)

    lf = _log_sigmoid(small + fb_ref[...])
    c = _cumsum_rows_scan(lf) + carry_ref[...]
    carry_ref[...] = c[tm - 1:tm, :]
    ct_ref[0] = jnp.transpose(c)[OFF_FF_IN_SMALL:OFF_FF_IN_SMALL + FOX_HEADS, :]
    c_hi, c_mid, c_lo = _split3(c * (-LOG2E))
    c_terms = jnp.where(lane < OFF_FF_IN_SMALL + FOX_HEADS, c_hi,
                        jnp.where(lane < OFF_FF_IN_SMALL + 2 * FOX_HEADS, c_mid, c_lo))
    c_place = _dot(c_terms, place_ref[...])

    ones = jnp.where((lane >= FOX_HD) & (lane < FOX_HD + N_SPLIT), 1.0, 0.0)

    def head_norm(t, g):
        ms = jnp.sum(t * t, axis=-1, keepdims=True) * (1.0 / FOX_HD)
        return t * lax.rsqrt(ms + EPS) * g

    fq = _dot(h, w_ref[:, OFF_FQ:OFF_FQ + FOX_SPREAD])
    fk = _dot(h, w_ref[:, OFF_FK:OFF_FK + FOX_SPREAD])
    for hd in range(FOX_HEADS):
        sl = slice(hd * LANES, (hd + 1) * LANES)
        fq_ref[:, sl] = (head_norm(fq[:, sl], qg_ref[:, sl]) + ones).astype(BF16)
        fk_ref[:, sl] = (head_norm(fk[:, sl], kg_ref[:, sl]) + c_place[:, sl]).astype(BF16)

    fvt = jnp.transpose(_dot(h, w_ref[:, OFF_FV:OFF_FV + FOX_WIDTH])).astype(BF16)
    for hd in range(FOX_HEADS):
        vt_ref[0, 0, hd * VT_ROWS:hd * VT_ROWS + FOX_HD, :] = fvt[hd * FOX_HD:(hd + 1) * FOX_HD, :]
        vt_ref[0, 0, hd * VT_ROWS + FOX_HD:(hd + 1) * VT_ROWS, :] = jnp.ones((BF16_ROWS, tm), BF16)

    gate_ref[...] = jax.nn.sigmoid(_dot(h, w_ref[:, OFF_GATE:]) + gb_ref[...]).astype(BF16)


def _proj_call(x2, g, w, a2, ab, fb, qg, kg, gb, place, batch, seq):
    tm = ROW_TILE
    n = seq // tm
    rows = lambda b, i: (b * n + i, 0)
    tokens = batch * seq
    return pl.pallas_call(
        _proj_kernel,
        grid=(batch, n),
        in_specs=[
            pl.BlockSpec((tm, D_MODEL), rows),
            _resident((1, D_MODEL)),
            _resident((D_MODEL, PACKED_COLS)),
            _resident((LANES, GLA_KW)),
            _resident((1, GLA_KW)),
            _resident((1, LANES)),
            _resident((1, FOX_SPREAD)),
            _resident((1, FOX_SPREAD)),
            _resident((1, 2 * D_MODEL)),
            _resident((LANES, FOX_SPREAD)),
        ],
        out_specs=[
            pl.BlockSpec((tm, SEC_GLA), rows),
            pl.BlockSpec((tm, GLA_KW), rows),
            pl.BlockSpec((tm, FOX_SPREAD), rows),
            pl.BlockSpec((tm, FOX_SPREAD), rows),
            pl.BlockSpec((1, 1, FOX_HEADS * VT_ROWS, tm), lambda b, i: (b, i, 0, 0)),
            pl.BlockSpec((1, FOX_HEADS, tm), lambda b, i: (b, 0, i)),
            pl.BlockSpec((tm, 2 * D_MODEL), rows),
        ],
        out_shape=[
            jax.ShapeDtypeStruct((tokens, SEC_GLA), BF16),
            jax.ShapeDtypeStruct((tokens, GLA_KW), F32),
            jax.ShapeDtypeStruct((tokens, FOX_SPREAD), BF16),
            jax.ShapeDtypeStruct((tokens, FOX_SPREAD), BF16),
            jax.ShapeDtypeStruct((batch, n, FOX_HEADS * VT_ROWS, tm), BF16),
            jax.ShapeDtypeStruct((batch, FOX_HEADS, seq), F32),
            jax.ShapeDtypeStruct((tokens, 2 * D_MODEL), BF16),
        ],
        scratch_shapes=[pltpu.VMEM((1, LANES), F32)],
        compiler_params=_GRID_PARAMS,
        name="proj",
    )(x2, g, w, a2, ab, fb, qg, kg, gb, place)


def _gla_kernel(gin_ref, la_ref, ng_ref, o_ref, st_ref):
    tm = gin_ref.shape[0]
    half = MXU_N

    @pl.when(pl.program_id(1) == 0)
    def _():
        st_ref[...] = jnp.zeros_like(st_ref)

    causal = _lower_tri(CHUNK)
    r_i = lax.broadcasted_iota(jnp.int32, (half, half), 0)
    c_i = lax.broadcasted_iota(jnp.int32, (half, half), 1)
    chunk_tri = ((r_i // CHUNK == c_i // CHUNK) & (r_i >= c_i)).astype(BF16)
    lane_head = lax.broadcasted_iota(jnp.int32, (1, GLA_KW), 1) // GLA_DK
    st_r = lax.broadcasted_iota(jnp.int32, (GLA_KW, GLA_WIDTH), 0) // GLA_DK
    st_c = lax.broadcasted_iota(jnp.int32, (GLA_KW, GLA_WIDTH), 1) // GLA_DV
    on_diag = st_r == st_c
    ng = ng_ref[...]

    work = []
    for hi in range(tm // half):
        b_half = _cumsum_rows_mxu(chunk_tri, la_ref[hi * half:(hi + 1) * half, :])
        for ci in range(half // CHUNK):
            rows = slice(hi * half + ci * CHUNK, hi * half + (ci + 1) * CHUNK)
            b = b_half[ci * CHUNK:(ci + 1) * CHUNK, :]
            b_last = b[CHUNK - 1:CHUNK, :]
            q = gin_ref[rows, 0:GLA_KW].astype(F32) * (GLA_DK ** -0.5)
            k = gin_ref[rows, GLA_KW:2 * GLA_KW].astype(F32)
            v = gin_ref[rows, 2 * GLA_KW:2 * GLA_KW + GLA_WIDTH]
            q_e = (q * jnp.exp(b)).astype(BF16)
            k_e = (k * jnp.exp(-b)).astype(BF16)
            k_d = (k * jnp.exp(b_last - b)).astype(BF16)
            parts = []
            for hd in range(GLA_HEADS):
                q_h = jnp.where(lane_head == hd, q_e, jnp.zeros_like(q_e))
                s = jnp.where(causal, _dot_nt(q_h, k_e), 0.0).astype(BF16)
                parts.append(_dot(s, v[:, hd * GLA_DV:(hd + 1) * GLA_DV]))
            o_intra = jnp.concatenate(parts, axis=1)
            kv = jnp.where(on_diag, _dot_tn(k_d, v), 0.0)
            dec = jnp.transpose(jnp.broadcast_to(jnp.exp(b_last), (LANES, GLA_KW)))
            work.append((rows, q_e, o_intra, kv, dec))

    state = st_ref[...]
    for rows, q_e, o_intra, kv, dec in work:
        o = o_intra + _dot(q_e, state.astype(BF16))
        state = jnp.concatenate([dec] * (GLA_WIDTH // LANES), axis=1) * state + kv
        r = gin_ref[rows, 2 * GLA_KW + GLA_WIDTH:].astype(F32)
        outs = []
        for hd in range(GLA_HEADS):
            sl = slice(hd * GLA_DV, (hd + 1) * GLA_DV)
            outs.append(_rms(o[:, sl], ng[:, sl]))
        o_ref[rows, :] = (jnp.concatenate(outs, axis=1) * (r * jax.nn.sigmoid(r))).astype(o_ref.dtype)
    st_ref[...] = state


def _gla_call(gin, la, ng, batch, seq):
    tm = ROW_TILE
    n = seq // tm
    rows = lambda b, i: (b * n + i, 0)
    return pl.pallas_call(
        _gla_kernel,
        grid=(batch, n),
        in_specs=[
            pl.BlockSpec((tm, SEC_GLA), rows),
            pl.BlockSpec((tm, GLA_KW), rows),
            pl.BlockSpec((1, GLA_WIDTH), lambda b, i: (0, 0)),
        ],
        out_specs=pl.BlockSpec((tm, GLA_WIDTH), rows),
        out_shape=jax.ShapeDtypeStruct((batch * seq, GLA_WIDTH), BF16),
        scratch_shapes=[pltpu.VMEM((GLA_KW, GLA_WIDTH), F32)],
        compiler_params=_GRID_PARAMS,
        name="gla",
    )(gin, la, ng)


def _fox_kernel(q_ref, k_ref, vt_ref, ct_ref, o_ref, s_ref, p_ref):
    grp = pl.program_id(1)
    seq = q_ref.shape[0]
    blk = vt_ref.shape[3]
    heads = range(FOX_GROUP)
    r_i = lax.broadcasted_iota(jnp.int32, (blk, blk), 0)
    c_i = lax.broadcasted_iota(jnp.int32, (blk, blk), 1)
    visible = r_i <= c_i

    def q_body(qi, carry):
        q0 = pl.multiple_of(qi * blk, blk)
        qs = [q_ref[pl.ds(q0, blk), j * LANES:(j + 1) * LANES] for j in heads]
        cqs = [ct_ref[0, FOX_GROUP * grp + j, pl.ds(qi, 1), :] * LOG2E for j in heads]

        def scores(ki):
            k0 = pl.multiple_of(ki * blk, blk)
            col_max = []
            for j in heads:
                s = _dot_nt(k_ref[pl.ds(k0, blk), j * LANES:(j + 1) * LANES], qs[j])
                s_ref[j] = s
                col_max.append(jnp.max(s, axis=0, keepdims=True))
            return col_max

        def probs(ms, col_max, masked):
            new_ms, alphas = [], []
            for j in heads:
                s = s_ref[j]
                if masked:
                    s = jnp.where(visible, s, NEG)
                    cm = jnp.max(s, axis=0, keepdims=True)
                else:
                    cm = col_max[j]
                m_new = jnp.maximum(ms[j], cm + cqs[j])
                p_ref[j] = jnp.exp2(s - (m_new - cqs[j])).astype(BF16)
                alphas.append(jnp.exp2(ms[j] - m_new))
                new_ms.append(m_new)
            return new_ms, alphas

        def accumulate(ki, accs, alphas):
            return [alphas[j] * accs[j] + _dot(vt_ref[0, ki, j * VT_ROWS:(j + 1) * VT_ROWS, :], p_ref[j])
                    for j in heads]

        def kv_step(ki, state):
            ms, accs, col_max = state
            ms, alphas = probs(ms, col_max, False)
            col_max = scores(ki + 1)
            accs = accumulate(ki, accs, alphas)
            return ms, accs, col_max

        init = ([jnp.full((1, blk), NEG, F32) for _ in heads],
                [jnp.zeros((VT_ROWS, blk), F32) for _ in heads],
                scores(0))
        ms, accs, col_max = lax.fori_loop(0, qi, kv_step, init)
        ms, alphas = probs(ms, col_max, True)
        accs = accumulate(qi, accs, alphas)
        o_t = jnp.concatenate([acc[0:FOX_HD] / acc[FOX_HD:FOX_HD + 1] for acc in accs], axis=0)
        o_ref[pl.ds(q0, blk), :] = jnp.transpose(o_t).astype(o_ref.dtype)
        return carry

    lax.fori_loop(0, seq // blk, q_body, 0)


def _fox_call(fq, fk, vt, ct4, batch, seq):
    blk = ROW_TILE
    groups = FOX_HEADS // FOX_GROUP
    return pl.pallas_call(
        _fox_kernel,
        grid=(batch, groups),
        in_specs=[
            pl.BlockSpec((seq, FOX_GROUP * LANES), lambda b, p: (b, p)),
            pl.BlockSpec((seq, FOX_GROUP * LANES), lambda b, p: (b, p)),
            pl.BlockSpec((1, seq // blk, FOX_GROUP * VT_ROWS, blk), lambda b, p: (b, 0, p, 0)),
            pl.BlockSpec((1, FOX_HEADS, seq // blk, blk), lambda b, p: (b, 0, 0, 0)),
        ],
        out_specs=pl.BlockSpec((seq, FOX_GROUP * FOX_HD), lambda b, p: (b, p)),
        out_shape=jax.ShapeDtypeStruct((batch * seq, FOX_WIDTH), BF16),
        scratch_shapes=[
            pltpu.VMEM((FOX_GROUP, blk, blk), F32),
            pltpu.VMEM((FOX_GROUP, blk, blk), BF16),
        ],
        compiler_params=_GRID_PARAMS,
        name="fox",
    )(fq, fk, vt, ct4)


def _gelu_tanh(a):
    return 0.5 * a * (1.0 + jnp.tanh(0.7978845608028654 * (a + 0.044715 * (a * a * a))))


def _ffn_kernel(x_ref, og_ref, of_ref, gate_ref, wg_ref, wf_ref, wo_ref, n2_ref,
                wu_ref, cw_ref, cb_ref, wd_ref, nf_ref, out_ref,
                carry_ref, stage_ref, act_ref):
    tm = x_ref.shape[0]

    @pl.when(pl.program_id(1) == 0)
    def _():
        carry_ref[...] = jnp.zeros_like(carry_ref)

    y = (gate_ref[:, 0:D_MODEL].astype(F32) * _dot(og_ref[...], wg_ref[...])
         + gate_ref[:, D_MODEL:].astype(F32) * _dot(of_ref[...], wf_ref[...]))
    x1 = x_ref[...] + _dot(y.astype(BF16), wo_ref[...])
    out_ref[...] = x1
    h2 = _rms(x1, n2_ref[...]).astype(BF16)

    for j in range(N_FF_CHUNKS):
        cols = slice(j * FF_CHUNK, (j + 1) * FF_CHUNK)
        ua = _dot(h2, wu_ref[:, cols])
        uv = _dot(h2, wu_ref[:, D_FF + j * FF_CHUNK:D_FF + (j + 1) * FF_CHUNK])
        slot = j % stage_ref.shape[0]
        stage_ref[slot, 0:SUBLANES, :] = carry_ref[:, cols]
        stage_ref[slot, SUBLANES:, :] = ua
        carry_ref[:, cols] = ua[tm - SUBLANES:, :]
        a = (cb_ref[:, cols] + cw_ref[0:1, cols] * stage_ref[slot, SUBLANES - 2:SUBLANES - 2 + tm, :]
             + cw_ref[1:2, cols] * stage_ref[slot, SUBLANES - 1:SUBLANES - 1 + tm, :] + cw_ref[2:3, cols] * ua)
        act_ref[:, cols] = (_gelu_tanh(a) * uv).astype(BF16)

    out_ref[...] = _rms(out_ref[...] + _dot(act_ref[...], wd_ref[...]), nf_ref[...])


def _ffn_call(x2, og, of, gate, wg, wf, wo, n2, wu, cw, cb, wd, nf, batch, seq):
    tm = ROW_TILE
    n = seq // tm
    rows = lambda b, i: (b * n + i, 0)
    return pl.pallas_call(
        _ffn_kernel,
        grid=(batch, n),
        in_specs=[
            pl.BlockSpec((tm, D_MODEL), rows),
            pl.BlockSpec((tm, GLA_WIDTH), rows),
            pl.BlockSpec((tm, FOX_WIDTH), rows),
            pl.BlockSpec((tm, 2 * D_MODEL), rows),
            _resident((GLA_WIDTH, D_MODEL)),
            _resident((FOX_WIDTH, D_MODEL)),
            _resident((D_MODEL, D_MODEL)),
            _resident((1, D_MODEL)),
            _resident((D_MODEL, 2 * D_FF)),
            _resident((SUBLANES, D_FF)),
            _resident((1, D_FF)),
            _resident((D_FF, D_MODEL)),
            _resident((1, D_MODEL)),
        ],
        out_specs=pl.BlockSpec((tm, D_MODEL), rows),
        out_shape=jax.ShapeDtypeStruct((batch * seq, D_MODEL), F32),
        scratch_shapes=[
            pltpu.VMEM((SUBLANES, D_FF), F32),
            pltpu.VMEM((2, tm + SUBLANES, FF_CHUNK), F32),
            pltpu.VMEM((tm, D_FF), BF16),
        ],
        compiler_params=_GRID_PARAMS,
        name="ffn",
    )(x2, og, of, gate, wg, wf, wo, n2, wu, cw, cb, wd, nf)


def _spread_heads(w):
    lead = w.shape[:-1]
    w = w.reshape(lead + (FOX_HEADS, FOX_HD))
    w = jnp.pad(w, [(0, 0)] * len(lead) + [(0, 0), (0, LANES - FOX_HD)])
    return w.reshape(lead + (FOX_SPREAD,))


def _layer(x2, batch, seq, norm_mix_g, w_in, gla_alpha_w2, gla_alpha_b, gla_out_norm_g,
           fox_forget_b, fox_q_norm_g, fox_k_norm_g, gate_b, w_gla_branch, w_fox_branch,
           w_out, norm_ffn_g, w_up, conv_w, conv_b, w_down, final_g):
    offs = [0]
    for s in IN_SPLITS:
        offs.append(offs[-1] + s)
    sec = lambda a, b: w_in[:, offs[a]:offs[b]]
    small_pad = LANES - LR_COPIES * GLA_LOWRANK - N_SPLIT * FOX_HEADS
    w_small = jnp.concatenate([sec(4, 5)] * LR_COPIES + [sec(8, 9)] * N_SPLIT
                              + [jnp.zeros((D_MODEL, small_pad), w_in.dtype)], axis=1)
    w_packed = jnp.concatenate([
        sec(0, 4), w_small, _spread_heads(sec(5, 6)), _spread_heads(sec(6, 7)), sec(7, 8), sec(9, 11),
    ], axis=1).astype(BF16)
    w2_hi, w2_mid, w2_lo = _split3(gla_alpha_w2)
    a2 = jnp.concatenate([w2_hi, w2_mid, w2_hi, w2_lo, w2_mid, w2_hi,
                          jnp.zeros((LANES - OFF_FF_IN_SMALL, GLA_KW), BF16)], axis=0)
    fb = jnp.concatenate([jnp.zeros((OFF_FF_IN_SMALL,), F32)] + [fox_forget_b] * N_SPLIT
                         + [jnp.zeros((small_pad,), F32)]).reshape(1, LANES)
    qg = _spread_heads(jnp.tile(fox_q_norm_g, FOX_HEADS) * (FOX_HD ** -0.5 * LOG2E)).reshape(1, FOX_SPREAD)
    kg = _spread_heads(jnp.tile(fox_k_norm_g, FOX_HEADS)).reshape(1, FOX_SPREAD)
    src = jnp.arange(LANES)[:, None] - OFF_FF_IN_SMALL
    col = jnp.arange(FOX_SPREAD)[None, :]
    place = ((src >= 0) & (src < N_SPLIT * FOX_HEADS)
             & (col == (src % FOX_HEADS) * LANES + FOX_HD + src // FOX_HEADS)).astype(BF16)

    gla_in, la, fq, fk, vt, ct, gates = _proj_call(
        x2, norm_mix_g.reshape(1, D_MODEL), w_packed, a2, gla_alpha_b.reshape(1, GLA_KW), fb, qg, kg,
        gate_b.reshape(1, 2 * D_MODEL), place, batch, seq)

    o_gla = _gla_call(gla_in, la, gla_out_norm_g.reshape(1, GLA_WIDTH), batch, seq)

    o_fox = _fox_call(fq, fk, vt, ct.reshape(batch, FOX_HEADS, seq // ROW_TILE, ROW_TILE), batch, seq)

    cw = jnp.pad(conv_w, ((0, SUBLANES - CONV_W), (0, 0)))
    return _ffn_call(
        x2, o_gla, o_fox, gates, w_gla_branch.astype(BF16), w_fox_branch.astype(BF16), w_out.astype(BF16),
        norm_ffn_g.reshape(1, D_MODEL), w_up.astype(BF16), cw, conv_b.reshape(1, D_FF), w_down.astype(BF16),
        final_g.reshape(1, D_MODEL), batch, seq)


def kernel(x, norm_mix_g, w_in, gla_alpha_w2, gla_alpha_b, gla_out_norm_g, fox_forget_b, fox_q_norm_g,
           fox_k_norm_g, gate_b, w_gla_branch, w_fox_branch, w_out, norm_ffn_g, w_up, conv_w, conv_b,
           w_down, norm_final_g):
    batch, seq, _ = x.shape
    assert norm_mix_g.shape[0] == 1, "the final RMSNorm is fused into the single layer's last kernel"
    assert seq % ROW_TILE == 0
    out = _layer(x.reshape(batch * seq, D_MODEL), batch, seq, norm_mix_g[0], w_in[0], gla_alpha_w2[0],
                 gla_alpha_b[0], gla_out_norm_g[0], fox_forget_b[0], fox_q_norm_g[0], fox_k_norm_g[0],
                 gate_b[0], w_gla_branch[0], w_fox_branch[0], w_out[0], norm_ffn_g[0], w_up[0],
                 conv_w[0], conv_b[0], w_down[0], norm_final_g)
    return out.reshape(batch, seq, D_MODEL)
```

```python
import jax
import jax.numpy as jnp
from jax import lax
from jax.experimental import pallas as pl
from jax.experimental.pallas import tpu as pltpu

D_MODEL = 1024
CHUNK = 64
GLA_HEADS = 4
GLA_DK = 64
GLA_DV = 128
GLA_LOWRANK = 16
GLA_TAU = 16.0
FOX_HEADS = 8
FOX_HD = 64
D_FF = 2816
CONV_W = 3
EPS = 1e-6

GLA_KW = GLA_HEADS * GLA_DK
GLA_WIDTH = GLA_HEADS * GLA_DV
FOX_WIDTH = FOX_HEADS * FOX_HD
IN_SPLITS = (GLA_KW, GLA_KW, GLA_WIDTH, GLA_WIDTH, GLA_LOWRANK,
             FOX_WIDTH, FOX_WIDTH, FOX_WIDTH, FOX_HEADS, D_MODEL, D_MODEL)

LANES = 128
SUBLANES = 8
BF16_ROWS = 16
MXU_N = 256
VMEM_LIMIT = 56 * 1024 * 1024

N_SPLIT = 3
LOG2E = 1.4426950408889634

SEC_GLA = 2 * GLA_KW + 2 * GLA_WIDTH
FOX_SPREAD = FOX_HEADS * LANES
LR_COPIES = 6
OFF_FF_IN_SMALL = LR_COPIES * GLA_LOWRANK
OFF_GLA = 0
OFF_SMALL = OFF_GLA + SEC_GLA
OFF_FQ = OFF_SMALL + LANES
OFF_FV = OFF_FQ + 2 * FOX_WIDTH
OFF_GATE = OFF_FV + FOX_WIDTH
PACKED_COLS = OFF_GATE + 2 * D_MODEL

ROW_TILE = 512
FF_CHUNK = MXU_N
N_FF_CHUNKS = D_FF // FF_CHUNK
FOX_GROUP = 4
VT_ROWS = FOX_HD + BF16_ROWS

BF16 = jnp.bfloat16
F32 = jnp.float32
NEG = -1e30


def _dot(a, b):
    return jnp.dot(a, b, preferred_element_type=F32)


def _dot_nt(a, b):
    return lax.dot_general(a, b, (((1,), (1,)), ((), ())), preferred_element_type=F32)


def _dot_tn(a, b):
    return lax.dot_general(a, b, (((0,), (0,)), ((), ())), preferred_element_type=F32)


def _rms(x, g):
    return x * lax.rsqrt(jnp.mean(x * x, axis=-1, keepdims=True) + EPS) * g


def _log_sigmoid(z):
    return jnp.minimum(z, 0.0) - jnp.log1p(jnp.exp(-jnp.abs(z)))


def _split3(a):
    hi = a.astype(BF16)
    r1 = a - hi.astype(F32)
    mid = r1.astype(BF16)
    lo = (r1 - mid.astype(F32)).astype(BF16)
    return hi, mid, lo


def _cumsum_rows_mxu(tri, a):
    hi, mid, lo = _split3(a)
    return _dot(tri, hi) + _dot(tri, mid) + _dot(tri, lo)


def _cumsum_rows_scan(x):
    n, w = x.shape
    row = lax.broadcasted_iota(jnp.int32, (n, w), 0)
    d = 1
    while d < n:
        if d < SUBLANES:
            shifted = jnp.where(row >= d, pltpu.roll(x, d, axis=0), 0.0)
        else:
            shifted = jnp.concatenate([jnp.zeros((d, w), F32), x[:n - d]], axis=0)
        x = x + shifted
        d *= 2
    return x


def _lower_tri(n):
    r = lax.broadcasted_iota(jnp.int32, (n, n), 0)
    c = lax.broadcasted_iota(jnp.int32, (n, n), 1)
    return r >= c


def _resident(shape):
    return pl.BlockSpec(shape, lambda b, i: (0,) * len(shape), pipeline_mode=pl.Buffered(1))


_GRID_PARAMS = pltpu.CompilerParams(
    dimension_semantics=("arbitrary", "arbitrary"), vmem_limit_bytes=VMEM_LIMIT)


def _proj_kernel(x_ref, g_ref, w_ref, a2_ref, ab_ref, fb_ref, qg_ref, kg_ref, gb_ref, place_ref,
                 gla_ref, la_ref, fq_ref, fk_ref, vt_ref, ct_ref, gate_ref, carry_ref):
    tm = x_ref.shape[0]

    @pl.when(pl.program_id(1) == 0)
    def _():
        carry_ref[...] = jnp.zeros_like(carry_ref)

    h = _rms(x_ref[...], g_ref[...]).astype(BF16)

    gla_ref[...] = _dot(h, w_ref[:, OFF_GLA:OFF_GLA + SEC_GLA]).astype(BF16)

    small = _dot(h, w_ref[:, OFF_SMALL:OFF_SMALL + LANES])
    lane = lax.broadcasted_iota(jnp.int32, (1, LANES), 1)

    s_hi, s_mid, s_lo = _split3(small)
    grp = lane // GLA_LOWRANK
    glr_terms = jnp.where((grp == 2) | (grp == 4), s_mid, jnp.where(grp == 5, s_lo, s_hi))
    z = _dot(glr_terms, a2_ref[...]) + ab_ref[...]
    la_ref[...] = _log_sigmoid(z) * (1.0 / GLA_TAU)

    lf = _log_sigmoid(small + fb_ref[...])
    c = _cumsum_rows_scan(lf) + carry_ref[...]
    carry_ref[...] = c[tm - 1:tm, :]
    ct_ref[0] = jnp.transpose(c)[OFF_FF_IN_SMALL:OFF_FF_IN_SMALL + FOX_HEADS, :]
    c_hi, c_mid, c_lo = _split3(c * (-LOG2E))
    c_terms = jnp.where(lane < OFF_FF_IN_SMALL + FOX_HEADS, c_hi,
                        jnp.where(lane < OFF_FF_IN_SMALL + 2 * FOX_HEADS, c_mid, c_lo))
    c_place = _dot(c_terms, place_ref[...])

    ones = jnp.where((lane >= FOX_HD) & (lane < FOX_HD + N_SPLIT), 1.0, 0.0)
    low = lane < FOX_HD

    def pair_norm(t, g):
        t2 = t * t
        ms_a = jnp.sum(jnp.where(low, t2, 0.0), axis=-1, keepdims=True) * (1.0 / FOX_HD)
        ms_b = jnp.sum(jnp.where(low, 0.0, t2), axis=-1, keepdims=True) * (1.0 / FOX_HD)
        return t * jnp.where(low, lax.rsqrt(ms_a + EPS), lax.rsqrt(ms_b + EPS)) * g

    fqk = _dot(h, w_ref[:, OFF_FQ:OFF_FQ + 2 * FOX_WIDTH])
    for pr in range(FOX_HEADS // 2):
        qn = pair_norm(fqk[:, pr * LANES:(pr + 1) * LANES], qg_ref[:, pr * LANES:(pr + 1) * LANES])
        kn = pair_norm(fqk[:, FOX_WIDTH + pr * LANES:FOX_WIDTH + (pr + 1) * LANES],
                       kg_ref[:, pr * LANES:(pr + 1) * LANES])
        for half, (qh, kh) in enumerate(((qn, kn), (pltpu.roll(qn, FOX_HD, axis=1), pltpu.roll(kn, FOX_HD, axis=1)))):
            sl = slice((2 * pr + half) * LANES, (2 * pr + half + 1) * LANES)
            fq_ref[:, sl] = jnp.where(low, qh, ones).astype(BF16)
            fk_ref[:, sl] = jnp.where(low, kh, c_place[:, sl]).astype(BF16)

    fvt = jnp.transpose(_dot(h, w_ref[:, OFF_FV:OFF_FV + FOX_WIDTH])).astype(BF16)
    for hd in range(FOX_HEADS):
        vt_ref[0, 0, hd * VT_ROWS:hd * VT_ROWS + FOX_HD, :] = fvt[hd * FOX_HD:(hd + 1) * FOX_HD, :]
        vt_ref[0, 0, hd * VT_ROWS + FOX_HD:(hd + 1) * VT_ROWS, :] = jnp.ones((BF16_ROWS, tm), BF16)

    gate_ref[...] = jax.nn.sigmoid(_dot(h, w_ref[:, OFF_GATE:]) + gb_ref[...]).astype(BF16)


def _proj_call(x2, g, w, a2, ab, fb, qg, kg, gb, place, batch, seq):
    tm = ROW_TILE
    n = seq // tm
    rows = lambda b, i: (b * n + i, 0)
    tokens = batch * seq
    return pl.pallas_call(
        _proj_kernel,
        grid=(batch, n),
        in_specs=[
            pl.BlockSpec((tm, D_MODEL), rows),
            _resident((1, D_MODEL)),
            _resident((D_MODEL, PACKED_COLS)),
            _resident((LANES, GLA_KW)),
            _resident((1, GLA_KW)),
            _resident((1, LANES)),
            _resident((1, FOX_WIDTH)),
            _resident((1, FOX_WIDTH)),
            _resident((1, 2 * D_MODEL)),
            _resident((LANES, FOX_SPREAD)),
        ],
        out_specs=[
            pl.BlockSpec((tm, SEC_GLA), rows),
            pl.BlockSpec((tm, GLA_KW), rows),
            pl.BlockSpec((tm, FOX_SPREAD), rows),
            pl.BlockSpec((tm, FOX_SPREAD), rows),
            pl.BlockSpec((1, 1, FOX_HEADS * VT_ROWS, tm), lambda b, i: (b, i, 0, 0)),
            pl.BlockSpec((1, FOX_HEADS, tm), lambda b, i: (b, 0, i)),
            pl.BlockSpec((tm, 2 * D_MODEL), rows),
        ],
        out_shape=[
            jax.ShapeDtypeStruct((tokens, SEC_GLA), BF16),
            jax.ShapeDtypeStruct((tokens, GLA_KW), F32),
            jax.ShapeDtypeStruct((tokens, FOX_SPREAD), BF16),
            jax.ShapeDtypeStruct((tokens, FOX_SPREAD), BF16),
            jax.ShapeDtypeStruct((batch, n, FOX_HEADS * VT_ROWS, tm), BF16),
            jax.ShapeDtypeStruct((batch, FOX_HEADS, seq), F32),
            jax.ShapeDtypeStruct((tokens, 2 * D_MODEL), BF16),
        ],
        scratch_shapes=[pltpu.VMEM((1, LANES), F32)],
        compiler_params=_GRID_PARAMS,
        name="proj",
    )(x2, g, w, a2, ab, fb, qg, kg, gb, place)


def _gla_kernel(gin_ref, la_ref, ng_ref, o_ref, st_ref):
    tm = gin_ref.shape[0]
    half = MXU_N
    chunks = range(tm // CHUNK)
    heads = range(GLA_HEADS)

    @pl.when(pl.program_id(1) == 0)
    def _():
        st_ref[...] = jnp.zeros_like(st_ref)

    causal = _lower_tri(CHUNK)
    r_i = lax.broadcasted_iota(jnp.int32, (half, half), 0)
    c_i = lax.broadcasted_iota(jnp.int32, (half, half), 1)
    chunk_tri = ((r_i // CHUNK == c_i // CHUNK) & (r_i >= c_i)).astype(BF16)
    lane_head = lax.broadcasted_iota(jnp.int32, (1, GLA_KW), 1) // GLA_DK
    ng = ng_ref[...]
    rows = [slice(c * CHUNK, (c + 1) * CHUNK) for c in chunks]
    vcol = [slice(2 * GLA_KW + hd * GLA_DV, 2 * GLA_KW + (hd + 1) * GLA_DV) for hd in heads]

    b_halves = [_cumsum_rows_mxu(chunk_tri, la_ref[hf * half:(hf + 1) * half, :]) for hf in range(tm // half)]
    per_half = half // CHUNK
    bs = [b_halves[c // per_half][(c % per_half) * CHUNK:(c % per_half + 1) * CHUNK, :] for c in chunks]

    q_e, k_e, k_d, dec = [], [], [], []
    for c in chunks:
        b_last = bs[c][CHUNK - 1:CHUNK, :]
        q = gin_ref[rows[c], 0:GLA_KW].astype(F32) * (GLA_DK ** -0.5)
        k = gin_ref[rows[c], GLA_KW:2 * GLA_KW].astype(F32)
        q_e.append((q * jnp.exp(bs[c])).astype(BF16))
        k_e.append((k * jnp.exp(-bs[c])).astype(BF16))
        k_d.append((k * jnp.exp(b_last - bs[c])).astype(BF16))
        dec.append(jnp.transpose(jnp.broadcast_to(jnp.exp(b_last), (LANES, GLA_KW))))

    q_h = [[jnp.where(lane_head == hd, q_e[c], jnp.zeros_like(q_e[c])) for hd in heads] for c in chunks]
    s = [[jnp.where(causal, _dot_nt(q_h[c][hd], k_e[c]), 0.0).astype(BF16) for hd in heads] for c in chunks]
    o_intra = [[_dot(s[c][hd], gin_ref[rows[c], vcol[hd]]) for hd in heads] for c in chunks]

    kv = []
    for c in chunks:
        full = _dot_tn(k_d[c], gin_ref[rows[c], 2 * GLA_KW:2 * GLA_KW + GLA_WIDTH])
        kv.append(jnp.concatenate(
            [full[hd * GLA_DK:(hd + 1) * GLA_DK, hd * GLA_DV:(hd + 1) * GLA_DV] for hd in heads], axis=0))

    state = st_ref[...]
    states = []
    for c in chunks:
        states.append(state.astype(BF16))
        state = dec[c] * state + kv[c]
    st_ref[...] = state

    for c in chunks:
        r = gin_ref[rows[c], 2 * GLA_KW + GLA_WIDTH:].astype(F32)
        outs = [_rms(o_intra[c][hd] + _dot(q_h[c][hd], states[c]), ng[:, hd * GLA_DV:(hd + 1) * GLA_DV])
                for hd in heads]
        o_ref[rows[c], :] = (jnp.concatenate(outs, axis=1) * (r * jax.nn.sigmoid(r))).astype(o_ref.dtype)


def _gla_call(gin, la, ng, batch, seq):
    tm = ROW_TILE
    n = seq // tm
    rows = lambda b, i: (b * n + i, 0)
    return pl.pallas_call(
        _gla_kernel,
        grid=(batch, n),
        in_specs=[
            pl.BlockSpec((tm, SEC_GLA), rows),
            pl.BlockSpec((tm, GLA_KW), rows),
            pl.BlockSpec((1, GLA_WIDTH), lambda b, i: (0, 0)),
        ],
        out_specs=pl.BlockSpec((tm, GLA_WIDTH), rows),
        out_shape=jax.ShapeDtypeStruct((batch * seq, GLA_WIDTH), BF16),
        scratch_shapes=[pltpu.VMEM((GLA_KW, GLA_DV), F32)],
        compiler_params=_GRID_PARAMS,
        name="gla",
    )(gin, la, ng)


def _fox_kernel(q_ref, k_ref, vt_ref, ct_ref, o_ref, s_ref, p_ref, acc_ref):
    grp = pl.program_id(1)
    seq = q_ref.shape[0]
    blk = vt_ref.shape[3]
    nq = seq // blk
    heads = range(FOX_GROUP)
    lanes = [slice(j * LANES, (j + 1) * LANES) for j in heads]

    def scores(qi, ki):
        q0 = pl.multiple_of(qi * blk, blk)
        k0 = pl.multiple_of(ki * blk, blk)
        col_max = []
        for j in heads:
            s = _dot_nt(k_ref[pl.ds(k0, blk), lanes[j]], q_ref[pl.ds(q0, blk), lanes[j]])
            s_ref[j] = s
            col_max.append(jnp.max(s, axis=0, keepdims=True))
        return col_max

    def probs(cqs, ms, col_max, diagonal):
        new_ms, alphas = [], []
        for j in heads:
            s = s_ref[j]
            if diagonal:
                s = jnp.where(lax.broadcasted_iota(jnp.int32, (blk, blk), 0)
                              <= lax.broadcasted_iota(jnp.int32, (blk, blk), 1), s, NEG)
                cm = jnp.max(s, axis=0, keepdims=True)
            else:
                cm = col_max[j]
            m_new = jnp.maximum(ms[j], cm + cqs[j])
            p_ref[j] = jnp.exp2(s - (m_new - cqs[j])).astype(BF16)
            alphas.append(jnp.exp2(ms[j] - m_new))
            new_ms.append(m_new)
        return new_ms, alphas

    def accumulate(ki, alphas):
        for j in heads:
            acc_ref[j] = alphas[j] * acc_ref[j] + _dot(vt_ref[0, ki, j * VT_ROWS:(j + 1) * VT_ROWS, :], p_ref[j])

    def q_body(qi, col_max):
        cqs = [ct_ref[0, FOX_GROUP * grp + j, pl.ds(qi, 1), :] * LOG2E for j in heads]
        acc_ref[...] = jnp.zeros_like(acc_ref)

        def kv_step(ki, state):
            ms, col_max = state
            ms, alphas = probs(cqs, ms, col_max, False)
            col_max = scores(qi, ki + 1)
            accumulate(ki, alphas)
            return ms, col_max

        ms, _ = lax.fori_loop(0, qi, kv_step, ([jnp.full((1, blk), NEG, F32) for _ in heads], col_max))
        ms, alphas = probs(cqs, ms, None, True)
        col_max = scores(jnp.minimum(qi + 1, nq - 1), 0)
        accumulate(qi, alphas)
        o_t = jnp.concatenate([acc_ref[j, 0:FOX_HD, :] / acc_ref[j, FOX_HD:FOX_HD + 1, :] for j in heads], axis=0)
        o_ref[pl.ds(pl.multiple_of(qi * blk, blk), blk), :] = jnp.transpose(o_t).astype(o_ref.dtype)
        return col_max

    lax.fori_loop(0, nq, q_body, scores(0, 0))


def _fox_call(fq, fk, vt, ct4, batch, seq):
    blk = ROW_TILE
    groups = FOX_HEADS // FOX_GROUP
    return pl.pallas_call(
        _fox_kernel,
        grid=(batch, groups),
        in_specs=[
            pl.BlockSpec((seq, FOX_GROUP * LANES), lambda b, p: (b, p)),
            pl.BlockSpec((seq, FOX_GROUP * LANES), lambda b, p: (b, p)),
            pl.BlockSpec((1, seq // blk, FOX_GROUP * VT_ROWS, blk), lambda b, p: (b, 0, p, 0)),
            pl.BlockSpec((1, FOX_HEADS, seq // blk, blk), lambda b, p: (b, 0, 0, 0)),
        ],
        out_specs=pl.BlockSpec((seq, FOX_GROUP * FOX_HD), lambda b, p: (b, p)),
        out_shape=jax.ShapeDtypeStruct((batch * seq, FOX_WIDTH), BF16),
        scratch_shapes=[
            pltpu.VMEM((FOX_GROUP, blk, blk), F32),
            pltpu.VMEM((FOX_GROUP, blk, blk), BF16),
            pltpu.VMEM((FOX_GROUP, VT_ROWS, blk), F32),
        ],
        compiler_params=_GRID_PARAMS,
        name="fox",
    )(fq, fk, vt, ct4)


def _gelu_tanh(a):
    return 0.5 * a * (1.0 + jnp.tanh(0.7978845608028654 * (a + 0.044715 * (a * a * a))))


def _ffn_kernel(x_ref, og_ref, of_ref, gate_ref, wg_ref, wf_ref, wo_ref, n2_ref,
                wu_ref, cw_ref, cb_ref, wd_ref, nf_ref, out_ref,
                carry_ref, stage_ref, act_ref):
    tm = x_ref.shape[0]

    @pl.when(pl.program_id(1) == 0)
    def _():
        carry_ref[...] = jnp.zeros_like(carry_ref)

    y = (gate_ref[:, 0:D_MODEL].astype(F32) * _dot(og_ref[...], wg_ref[...])
         + gate_ref[:, D_MODEL:].astype(F32) * _dot(of_ref[...], wf_ref[...]))
    x1 = x_ref[...] + _dot(y.astype(BF16), wo_ref[...])
    out_ref[...] = x1
    h2 = _rms(x1, n2_ref[...]).astype(BF16)

    for j in range(N_FF_CHUNKS):
        cols = slice(j * FF_CHUNK, (j + 1) * FF_CHUNK)
        ua = _dot(h2, wu_ref[:, cols])
        uv = _dot(h2, wu_ref[:, D_FF + j * FF_CHUNK:D_FF + (j + 1) * FF_CHUNK])
        slot = j % stage_ref.shape[0]
        stage_ref[slot, 0:SUBLANES, :] = carry_ref[:, cols]
        stage_ref[slot, SUBLANES:, :] = ua
        carry_ref[:, cols] = ua[tm - SUBLANES:, :]
        a = (cb_ref[:, cols] + cw_ref[0:1, cols] * stage_ref[slot, SUBLANES - 2:SUBLANES - 2 + tm, :]
             + cw_ref[1:2, cols] * stage_ref[slot, SUBLANES - 1:SUBLANES - 1 + tm, :] + cw_ref[2:3, cols] * ua)
        act_ref[:, cols] = (_gelu_tanh(a) * uv).astype(BF16)

    out_ref[...] = _rms(out_ref[...] + _dot(act_ref[...], wd_ref[...]), nf_ref[...])


def _ffn_call(x2, og, of, gate, wg, wf, wo, n2, wu, cw, cb, wd, nf, batch, seq):
    tm = ROW_TILE
    n = seq // tm
    rows = lambda b, i: (b * n + i, 0)
    return pl.pallas_call(
        _ffn_kernel,
        grid=(batch, n),
        in_specs=[
            pl.BlockSpec((tm, D_MODEL), rows),
            pl.BlockSpec((tm, GLA_WIDTH), rows),
            pl.BlockSpec((tm, FOX_WIDTH), rows),
            pl.BlockSpec((tm, 2 * D_MODEL), rows),
            _resident((GLA_WIDTH, D_MODEL)),
            _resident((FOX_WIDTH, D_MODEL)),
            _resident((D_MODEL, D_MODEL)),
            _resident((1, D_MODEL)),
            _resident((D_MODEL, 2 * D_FF)),
            _resident((SUBLANES, D_FF)),
            _resident((1, D_FF)),
            _resident((D_FF, D_MODEL)),
            _resident((1, D_MODEL)),
        ],
        out_specs=pl.BlockSpec((tm, D_MODEL), rows),
        out_shape=jax.ShapeDtypeStruct((batch * seq, D_MODEL), F32),
        scratch_shapes=[
            pltpu.VMEM((SUBLANES, D_FF), F32),
            pltpu.VMEM((2, tm + SUBLANES, FF_CHUNK), F32),
            pltpu.VMEM((tm, D_FF), BF16),
        ],
        compiler_params=_GRID_PARAMS,
        name="ffn",
    )(x2, og, of, gate, wg, wf, wo, n2, wu, cw, cb, wd, nf)


def _layer(x2, batch, seq, norm_mix_g, w_in, gla_alpha_w2, gla_alpha_b, gla_out_norm_g,
           fox_forget_b, fox_q_norm_g, fox_k_norm_g, gate_b, w_gla_branch, w_fox_branch,
           w_out, norm_ffn_g, w_up, conv_w, conv_b, w_down, final_g):
    offs = [0]
    for s in IN_SPLITS:
        offs.append(offs[-1] + s)
    sec = lambda a, b: w_in[:, offs[a]:offs[b]]
    small_pad = LANES - LR_COPIES * GLA_LOWRANK - N_SPLIT * FOX_HEADS
    w_small = jnp.concatenate([sec(4, 5)] * LR_COPIES + [sec(8, 9)] * N_SPLIT
                              + [jnp.zeros((D_MODEL, small_pad), w_in.dtype)], axis=1)
    w_packed = jnp.concatenate([
        sec(0, 4), w_small, sec(5, 8), sec(9, 11),
    ], axis=1).astype(BF16)
    w2_hi, w2_mid, w2_lo = _split3(gla_alpha_w2)
    a2 = jnp.concatenate([w2_hi, w2_mid, w2_hi, w2_lo, w2_mid, w2_hi,
                          jnp.zeros((LANES - OFF_FF_IN_SMALL, GLA_KW), BF16)], axis=0)
    fb = jnp.concatenate([jnp.zeros((OFF_FF_IN_SMALL,), F32)] + [fox_forget_b] * N_SPLIT
                         + [jnp.zeros((small_pad,), F32)]).reshape(1, LANES)
    qg = (jnp.tile(fox_q_norm_g, FOX_HEADS) * (FOX_HD ** -0.5 * LOG2E)).reshape(1, FOX_WIDTH)
    kg = jnp.tile(fox_k_norm_g, FOX_HEADS).reshape(1, FOX_WIDTH)
    src = jnp.arange(LANES)[:, None] - OFF_FF_IN_SMALL
    col = jnp.arange(FOX_SPREAD)[None, :]
    place = ((src >= 0) & (src < N_SPLIT * FOX_HEADS)
             & (col == (src % FOX_HEADS) * LANES + FOX_HD + src // FOX_HEADS)).astype(BF16)

    gla_in, la, fq, fk, vt, ct, gates = _proj_call(
        x2, norm_mix_g.reshape(1, D_MODEL), w_packed, a2, gla_alpha_b.reshape(1, GLA_KW), fb, qg, kg,
        gate_b.reshape(1, 2 * D_MODEL), place, batch, seq)

    o_gla = _gla_call(gla_in, la, gla_out_norm_g.reshape(1, GLA_WIDTH), batch, seq)

    o_fox = _fox_call(fq, fk, vt, ct.reshape(batch, FOX_HEADS, seq // ROW_TILE, ROW_TILE), batch, seq)

    cw = jnp.pad(conv_w, ((0, SUBLANES - CONV_W), (0, 0)))
    return _ffn_call(
        x2, o_gla, o_fox, gates, w_gla_branch.astype(BF16), w_fox_branch.astype(BF16), w_out.astype(BF16),
        norm_ffn_g.reshape(1, D_MODEL), w_up.astype(BF16), cw, conv_b.reshape(1, D_FF), w_down.astype(BF16),
        final_g.reshape(1, D_MODEL), batch, seq)


def kernel(x, norm_mix_g, w_in, gla_alpha_w2, gla_alpha_b, gla_out_norm_g, fox_forget_b, fox_q_norm_g,
           fox_k_norm_g, gate_b, w_gla_branch, w_fox_branch, w_out, norm_ffn_g, w_up, conv_w, conv_b,
           w_down, norm_final_g):
    batch, seq, _ = x.shape
    assert norm_mix_g.shape[0] == 1, "the final RMSNorm is fused into the single layer's last kernel"
    assert seq % ROW_TILE == 0
    out = _layer(x.reshape(batch * seq, D_MODEL), batch, seq, norm_mix_g[0], w_in[0], gla_alpha_w2[0],
                 gla_alpha_b[0], gla_out_norm_g[0], fox_forget_b[0], fox_q_norm_g[0], fox_k_norm_g[0],
                 gate_b[0], w_gla_branch[0], w_fox_branch[0], w_out[0], norm_ffn_g[0], w_up[0],
                 conv_w[0], conv_b[0], w_down[0], norm_final_g)
    return out.reshape(batch, seq, D_MODEL)
```

```python
import jax
import jax.numpy as jnp
from jax import lax
from jax.experimental import pallas as pl
from jax.experimental.pallas import tpu as pltpu

D_MODEL = 1024
CHUNK = 64
GLA_HEADS = 4
GLA_DK = 64
GLA_DV = 128
GLA_LOWRANK = 16
GLA_TAU = 16.0
FOX_HEADS = 8
FOX_HD = 64
D_FF = 2816
CONV_W = 3
EPS = 1e-6

GLA_KW = GLA_HEADS * GLA_DK
GLA_WIDTH = GLA_HEADS * GLA_DV
FOX_WIDTH = FOX_HEADS * FOX_HD
IN_SPLITS = (GLA_KW, GLA_KW, GLA_WIDTH, GLA_WIDTH, GLA_LOWRANK,
             FOX_WIDTH, FOX_WIDTH, FOX_WIDTH, FOX_HEADS, D_MODEL, D_MODEL)

LANES = 128
SUBLANES = 8
BF16_ROWS = 16
MXU_N = 256
VMEM_LIMIT = 56 * 1024 * 1024

N_SPLIT = 3
LOG2E = 1.4426950408889634

SEC_GLA = 2 * GLA_KW + 2 * GLA_WIDTH
FOX_SPREAD = FOX_HEADS * LANES
LR_COPIES = 6
OFF_FF_IN_SMALL = LR_COPIES * GLA_LOWRANK
OFF_GLA = 0
OFF_SMALL = OFF_GLA + SEC_GLA
OFF_FQ = OFF_SMALL + LANES
OFF_FV = OFF_FQ + 2 * FOX_WIDTH
OFF_GATE = OFF_FV + FOX_WIDTH
PACKED_COLS = OFF_GATE + 2 * D_MODEL

ROW_TILE = 512
FF_CHUNK = MXU_N
N_FF_CHUNKS = D_FF // FF_CHUNK
FOX_GROUP = 4
VT_ROWS = FOX_HD + BF16_ROWS

BF16 = jnp.bfloat16
F32 = jnp.float32
NEG = -1e30


def _dot(a, b):
    return jnp.dot(a, b, preferred_element_type=F32)


def _dot_nt(a, b):
    return lax.dot_general(a, b, (((1,), (1,)), ((), ())), preferred_element_type=F32)


def _dot_tn(a, b):
    return lax.dot_general(a, b, (((0,), (0,)), ((), ())), preferred_element_type=F32)


def _rms(x, g):
    return x * lax.rsqrt(jnp.mean(x * x, axis=-1, keepdims=True) + EPS) * g


def _log_sigmoid(z):
    return jnp.minimum(z, 0.0) - jnp.log1p(jnp.exp(-jnp.abs(z)))


def _split3(a):
    hi = a.astype(BF16)
    r1 = a - hi.astype(F32)
    mid = r1.astype(BF16)
    lo = (r1 - mid.astype(F32)).astype(BF16)
    return hi, mid, lo


def _cumsum_rows_mxu(tri, a):
    hi, mid, lo = _split3(a)
    return _dot(tri, hi) + _dot(tri, mid) + _dot(tri, lo)


def _cumsum_rows_scan(x):
    n, w = x.shape
    row = lax.broadcasted_iota(jnp.int32, (n, w), 0)
    d = 1
    while d < n:
        if d < SUBLANES:
            shifted = jnp.where(row >= d, pltpu.roll(x, d, axis=0), 0.0)
        else:
            shifted = jnp.concatenate([jnp.zeros((d, w), F32), x[:n - d]], axis=0)
        x = x + shifted
        d *= 2
    return x


def _lower_tri(n):
    r = lax.broadcasted_iota(jnp.int32, (n, n), 0)
    c = lax.broadcasted_iota(jnp.int32, (n, n), 1)
    return r >= c


def _resident(shape):
    return pl.BlockSpec(shape, lambda b, i: (0,) * len(shape), pipeline_mode=pl.Buffered(1))


_GRID_PARAMS = pltpu.CompilerParams(
    dimension_semantics=("arbitrary", "arbitrary"), vmem_limit_bytes=VMEM_LIMIT)


def _proj_kernel(x_ref, g_ref, w_ref, a2_ref, ab_ref, fb_ref, qg_ref, kg_ref, gb_ref, place_ref,
                 gla_ref, la_ref, fq_ref, fk_ref, vt_ref, ct_ref, gate_ref, carry_ref):
    tm = x_ref.shape[0]

    @pl.when(pl.program_id(1) == 0)
    def _():
        carry_ref[...] = jnp.zeros_like(carry_ref)

    h = _rms(x_ref[...], g_ref[...]).astype(BF16)

    small = _dot(h, w_ref[:, OFF_SMALL:OFF_SMALL + LANES])
    gate_ref[...] = jax.nn.sigmoid(_dot(h, w_ref[:, OFF_GATE:]) + gb_ref[...]).astype(BF16)
    lane = lax.broadcasted_iota(jnp.int32, (1, LANES), 1)

    s_hi, s_mid, s_lo = _split3(small)
    grp = lane // GLA_LOWRANK
    glr_terms = jnp.where((grp == 2) | (grp == 4), s_mid, jnp.where(grp == 5, s_lo, s_hi))
    z = _dot(glr_terms, a2_ref[...]) + ab_ref[...]
    la_ref[...] = _log_sigmoid(z) * (1.0 / GLA_TAU)

    lf = _log_sigmoid(small + fb_ref[...])
    c = _cumsum_rows_scan(lf) + carry_ref[...]
    carry_ref[...] = c[tm - 1:tm, :]
    ct_ref[0] = jnp.transpose(c)[OFF_FF_IN_SMALL:OFF_FF_IN_SMALL + FOX_HEADS, :]
    c_hi, c_mid, c_lo = _split3(c * (-LOG2E))
    c_terms = jnp.where(lane < OFF_FF_IN_SMALL + FOX_HEADS, c_hi,
                        jnp.where(lane < OFF_FF_IN_SMALL + 2 * FOX_HEADS, c_mid, c_lo))
    c_place = _dot(c_terms, place_ref[...])

    ones = jnp.where((lane >= FOX_HD) & (lane < FOX_HD + N_SPLIT), 1.0, 0.0)
    low = lane < FOX_HD

    def pair_norm(t, g):
        t2 = t * t
        ms_a = jnp.sum(jnp.where(low, t2, 0.0), axis=-1, keepdims=True) * (1.0 / FOX_HD)
        ms_b = jnp.sum(jnp.where(low, 0.0, t2), axis=-1, keepdims=True) * (1.0 / FOX_HD)
        return t * jnp.where(low, lax.rsqrt(ms_a + EPS), lax.rsqrt(ms_b + EPS)) * g

    fqk = _dot(h, w_ref[:, OFF_FQ:OFF_FQ + 2 * FOX_WIDTH])
    for pr in range(FOX_HEADS // 2):
        qn = pair_norm(fqk[:, pr * LANES:(pr + 1) * LANES], qg_ref[:, pr * LANES:(pr + 1) * LANES])
        kn = pair_norm(fqk[:, FOX_WIDTH + pr * LANES:FOX_WIDTH + (pr + 1) * LANES],
                       kg_ref[:, pr * LANES:(pr + 1) * LANES])
        for half, (qh, kh) in enumerate(((qn, kn), (pltpu.roll(qn, FOX_HD, axis=1), pltpu.roll(kn, FOX_HD, axis=1)))):
            sl = slice((2 * pr + half) * LANES, (2 * pr + half + 1) * LANES)
            fq_ref[:, sl] = jnp.where(low, qh, ones).astype(BF16)
            fk_ref[:, sl] = jnp.where(low, kh, c_place[:, sl]).astype(BF16)

    fvt = jnp.transpose(_dot(h, w_ref[:, OFF_FV:OFF_FV + FOX_WIDTH])).astype(BF16)
    for hd in range(FOX_HEADS):
        vt_ref[0, 0, hd * VT_ROWS:hd * VT_ROWS + FOX_HD, :] = fvt[hd * FOX_HD:(hd + 1) * FOX_HD, :]
        vt_ref[0, 0, hd * VT_ROWS + FOX_HD:(hd + 1) * VT_ROWS, :] = jnp.ones((BF16_ROWS, tm), BF16)

    gla_ref[...] = _dot(h, w_ref[:, OFF_GLA:OFF_GLA + SEC_GLA]).astype(BF16)


def _proj_call(x2, g, w, a2, ab, fb, qg, kg, gb, place, batch, seq):
    tm = ROW_TILE
    n = seq // tm
    rows = lambda b, i: (b * n + i, 0)
    tokens = batch * seq
    return pl.pallas_call(
        _proj_kernel,
        grid=(batch, n),
        in_specs=[
            pl.BlockSpec((tm, D_MODEL), rows),
            _resident((1, D_MODEL)),
            _resident((D_MODEL, PACKED_COLS)),
            _resident((LANES, GLA_KW)),
            _resident((1, GLA_KW)),
            _resident((1, LANES)),
            _resident((1, FOX_WIDTH)),
            _resident((1, FOX_WIDTH)),
            _resident((1, 2 * D_MODEL)),
            _resident((LANES, FOX_SPREAD)),
        ],
        out_specs=[
            pl.BlockSpec((tm, SEC_GLA), rows),
            pl.BlockSpec((tm, GLA_KW), rows),
            pl.BlockSpec((tm, FOX_SPREAD), rows),
            pl.BlockSpec((tm, FOX_SPREAD), rows),
            pl.BlockSpec((1, 1, FOX_HEADS * VT_ROWS, tm), lambda b, i: (b, i, 0, 0)),
            pl.BlockSpec((1, FOX_HEADS, tm), lambda b, i: (b, 0, i)),
            pl.BlockSpec((tm, 2 * D_MODEL), rows),
        ],
        out_shape=[
            jax.ShapeDtypeStruct((tokens, SEC_GLA), BF16),
            jax.ShapeDtypeStruct((tokens, GLA_KW), F32),
            jax.ShapeDtypeStruct((tokens, FOX_SPREAD), BF16),
            jax.ShapeDtypeStruct((tokens, FOX_SPREAD), BF16),
            jax.ShapeDtypeStruct((batch, n, FOX_HEADS * VT_ROWS, tm), BF16),
            jax.ShapeDtypeStruct((batch, FOX_HEADS, seq), F32),
            jax.ShapeDtypeStruct((tokens, 2 * D_MODEL), BF16),
        ],
        scratch_shapes=[pltpu.VMEM((1, LANES), F32)],
        compiler_params=_GRID_PARAMS,
        name="proj",
    )(x2, g, w, a2, ab, fb, qg, kg, gb, place)


def _gla_kernel(gin_ref, la_ref, ng_ref, o_ref, st_ref):
    tm = gin_ref.shape[0]
    half = MXU_N
    chunks = range(tm // CHUNK)
    heads = range(GLA_HEADS)

    @pl.when(pl.program_id(1) == 0)
    def _():
        st_ref[...] = jnp.zeros_like(st_ref)

    causal = _lower_tri(CHUNK)
    r_i = lax.broadcasted_iota(jnp.int32, (half, half), 0)
    c_i = lax.broadcasted_iota(jnp.int32, (half, half), 1)
    chunk_tri = ((r_i // CHUNK == c_i // CHUNK) & (r_i >= c_i)).astype(BF16)
    lane_head = lax.broadcasted_iota(jnp.int32, (1, GLA_KW), 1) // GLA_DK
    ng = ng_ref[...]
    rows = [slice(c * CHUNK, (c + 1) * CHUNK) for c in chunks]
    vcol = [slice(2 * GLA_KW + hd * GLA_DV, 2 * GLA_KW + (hd + 1) * GLA_DV) for hd in heads]

    b_halves = [_cumsum_rows_mxu(chunk_tri, la_ref[hf * half:(hf + 1) * half, :]) for hf in range(tm // half)]
    per_half = half // CHUNK
    bs = [b_halves[c // per_half][(c % per_half) * CHUNK:(c % per_half + 1) * CHUNK, :] for c in chunks]

    q_e, k_e, k_d, dec = [], [], [], []
    for c in chunks:
        b_last = bs[c][CHUNK - 1:CHUNK, :]
        q = gin_ref[rows[c], 0:GLA_KW].astype(F32) * (GLA_DK ** -0.5)
        k = gin_ref[rows[c], GLA_KW:2 * GLA_KW].astype(F32)
        q_e.append((q * jnp.exp(bs[c])).astype(BF16))
        k_e.append((k * jnp.exp(-bs[c])).astype(BF16))
        k_d.append((k * jnp.exp(b_last - bs[c])).astype(BF16))
        dec.append(jnp.transpose(jnp.broadcast_to(jnp.exp(b_last), (LANES, GLA_KW))))

    q_h = [[jnp.where(lane_head == hd, q_e[c], jnp.zeros_like(q_e[c])) for hd in heads] for c in chunks]
    s = [[jnp.where(causal, _dot_nt(q_h[c][hd], k_e[c]), 0.0).astype(BF16) for hd in heads] for c in chunks]
    o_intra = [[_dot(s[c][hd], gin_ref[rows[c], vcol[hd]]) for hd in heads] for c in chunks]

    kv = []
    for c in chunks:
        full = _dot_tn(k_d[c], gin_ref[rows[c], 2 * GLA_KW:2 * GLA_KW + GLA_WIDTH])
        kv.append(jnp.concatenate(
            [full[hd * GLA_DK:(hd + 1) * GLA_DK, hd * GLA_DV:(hd + 1) * GLA_DV] for hd in heads], axis=0))

    state = st_ref[...]
    states = []
    for c in chunks:
        states.append(state.astype(BF16))
        state = dec[c] * state + kv[c]
    st_ref[...] = state

    for c in chunks:
        r = gin_ref[rows[c], 2 * GLA_KW + GLA_WIDTH:].astype(F32)
        outs = [_rms(o_intra[c][hd] + _dot(q_h[c][hd], states[c]), ng[:, hd * GLA_DV:(hd + 1) * GLA_DV])
                for hd in heads]
        o_ref[rows[c], :] = (jnp.concatenate(outs, axis=1) * (r * jax.nn.sigmoid(r))).astype(o_ref.dtype)


def _gla_call(gin, la, ng, batch, seq):
    tm = ROW_TILE
    n = seq // tm
    rows = lambda b, i: (b * n + i, 0)
    return pl.pallas_call(
        _gla_kernel,
        grid=(batch, n),
        in_specs=[
            pl.BlockSpec((tm, SEC_GLA), rows),
            pl.BlockSpec((tm, GLA_KW), rows),
            pl.BlockSpec((1, GLA_WIDTH), lambda b, i: (0, 0)),
        ],
        out_specs=pl.BlockSpec((tm, GLA_WIDTH), rows),
        out_shape=jax.ShapeDtypeStruct((batch * seq, GLA_WIDTH), BF16),
        scratch_shapes=[pltpu.VMEM((GLA_KW, GLA_DV), F32)],
        compiler_params=_GRID_PARAMS,
        name="gla",
    )(gin, la, ng)


def _fox_kernel(q_ref, k_ref, vt_ref, ct_ref, o_ref, s_ref, p_ref, acc_ref):
    grp = pl.program_id(1)
    seq = q_ref.shape[0]
    blk = vt_ref.shape[3]
    nq = seq // blk
    heads = range(FOX_GROUP)
    lanes = [slice(j * LANES, (j + 1) * LANES) for j in heads]

    def scores(qi, ki):
        q0 = pl.multiple_of(qi * blk, blk)
        k0 = pl.multiple_of(ki * blk, blk)
        col_max = []
        for j in heads:
            s = _dot_nt(k_ref[pl.ds(k0, blk), lanes[j]], q_ref[pl.ds(q0, blk), lanes[j]])
            s_ref[j] = s
            col_max.append(jnp.max(s, axis=0, keepdims=True))
        return col_max

    def probs(cqs, ms, col_max):
        new_ms, alphas = [], []
        for j in heads:
            m_new = jnp.maximum(ms[j], col_max[j] + cqs[j])
            p_ref[j] = jnp.exp2(s_ref[j] - (m_new - cqs[j])).astype(BF16)
            alphas.append(jnp.exp2(ms[j] - m_new))
            new_ms.append(m_new)
        return new_ms, alphas

    def accumulate(ki, alphas):
        for j in heads:
            acc_ref[j] = alphas[j] * acc_ref[j] + _dot(vt_ref[0, ki, j * VT_ROWS:(j + 1) * VT_ROWS, :], p_ref[j])

    hb = blk // 2
    lo, hi = slice(0, hb), slice(hb, blk)

    def diagonal_probs(cqs, ms):
        tri = (lax.broadcasted_iota(jnp.int32, (hb, hb), 0)
               <= lax.broadcasted_iota(jnp.int32, (hb, hb), 1))
        alphas = []
        for j in heads:
            s00 = jnp.where(tri, s_ref[j, lo, lo], NEG)
            s01 = s_ref[j, lo, hi]
            s11 = jnp.where(tri, s_ref[j, hi, hi], NEG)
            cm = jnp.concatenate(
                [jnp.max(s00, axis=0, keepdims=True),
                 jnp.maximum(jnp.max(s01, axis=0, keepdims=True), jnp.max(s11, axis=0, keepdims=True))], axis=1)
            m_new = jnp.maximum(ms[j], cm + cqs[j])
            ref = m_new - cqs[j]
            p_ref[j, lo, lo] = jnp.exp2(s00 - ref[:, lo]).astype(BF16)
            p_ref[j, lo, hi] = jnp.exp2(s01 - ref[:, hi]).astype(BF16)
            p_ref[j, hi, hi] = jnp.exp2(s11 - ref[:, hi]).astype(BF16)
            alphas.append(jnp.exp2(ms[j] - m_new))
        return alphas

    def diagonal_accumulate(ki, alphas):
        for j in heads:
            vt = vt_ref[0, ki, j * VT_ROWS:(j + 1) * VT_ROWS, :]
            acc_ref[j, :, lo] = alphas[j][:, lo] * acc_ref[j, :, lo] + _dot(vt[:, lo], p_ref[j, lo, lo])
            acc_ref[j, :, hi] = alphas[j][:, hi] * acc_ref[j, :, hi] + _dot(vt, p_ref[j, :, hi])

    def q_body(qi, col_max):
        cqs = [ct_ref[0, FOX_GROUP * grp + j, pl.ds(qi, 1), :] * LOG2E for j in heads]
        acc_ref[...] = jnp.zeros_like(acc_ref)

        def kv_step(ki, state):
            ms, col_max = state
            ms, alphas = probs(cqs, ms, col_max)
            col_max = scores(qi, ki + 1)
            accumulate(ki, alphas)
            return ms, col_max

        ms, _ = lax.fori_loop(0, qi, kv_step, ([jnp.full((1, blk), NEG, F32) for _ in heads], col_max))
        alphas = diagonal_probs(cqs, ms)
        col_max = scores(jnp.minimum(qi + 1, nq - 1), 0)
        diagonal_accumulate(qi, alphas)
        o_t = jnp.concatenate([acc_ref[j, 0:FOX_HD, :] / acc_ref[j, FOX_HD:FOX_HD + 1, :] for j in heads], axis=0)
        o_ref[pl.ds(pl.multiple_of(qi * blk, blk), blk), :] = jnp.transpose(o_t).astype(o_ref.dtype)
        return col_max

    lax.fori_loop(0, nq, q_body, scores(0, 0))


def _fox_call(fq, fk, vt, ct4, batch, seq):
    blk = ROW_TILE
    groups = FOX_HEADS // FOX_GROUP
    return pl.pallas_call(
        _fox_kernel,
        grid=(batch, groups),
        in_specs=[
            pl.BlockSpec((seq, FOX_GROUP * LANES), lambda b, p: (b, p)),
            pl.BlockSpec((seq, FOX_GROUP * LANES), lambda b, p: (b, p)),
            pl.BlockSpec((1, seq // blk, FOX_GROUP * VT_ROWS, blk), lambda b, p: (b, 0, p, 0)),
            pl.BlockSpec((1, FOX_HEADS, seq // blk, blk), lambda b, p: (b, 0, 0, 0)),
        ],
        out_specs=pl.BlockSpec((seq, FOX_GROUP * FOX_HD), lambda b, p: (b, p)),
        out_shape=jax.ShapeDtypeStruct((batch * seq, FOX_WIDTH), BF16),
        scratch_shapes=[
            pltpu.VMEM((FOX_GROUP, blk, blk), F32),
            pltpu.VMEM((FOX_GROUP, blk, blk), BF16),
            pltpu.VMEM((FOX_GROUP, VT_ROWS, blk), F32),
        ],
        compiler_params=_GRID_PARAMS,
        name="fox",
    )(fq, fk, vt, ct4)


def _gelu_tanh(a):
    return 0.5 * a * (1.0 + jnp.tanh(0.7978845608028654 * (a + 0.044715 * (a * a * a))))


def _ffn_kernel(x_ref, og_ref, of_ref, gate_ref, wg_ref, wf_ref, wo_ref, n2_ref,
                wu_ref, cw_ref, cb_ref, wd_ref, nf_ref, out_ref,
                carry_ref, stage_ref, act_ref):
    tm = x_ref.shape[0]

    @pl.when(pl.program_id(1) == 0)
    def _():
        carry_ref[...] = jnp.zeros_like(carry_ref)

    y = (gate_ref[:, 0:D_MODEL].astype(F32) * _dot(og_ref[...], wg_ref[...])
         + gate_ref[:, D_MODEL:].astype(F32) * _dot(of_ref[...], wf_ref[...]))
    x1 = x_ref[...] + _dot(y.astype(BF16), wo_ref[...])
    out_ref[...] = x1
    h2 = _rms(x1, n2_ref[...]).astype(BF16)

    for j in range(N_FF_CHUNKS):
        cols = slice(j * FF_CHUNK, (j + 1) * FF_CHUNK)
        ua = _dot(h2, wu_ref[:, cols])
        uv = _dot(h2, wu_ref[:, D_FF + j * FF_CHUNK:D_FF + (j + 1) * FF_CHUNK])
        slot = j % stage_ref.shape[0]
        stage_ref[slot, 0:SUBLANES, :] = carry_ref[:, cols]
        stage_ref[slot, SUBLANES:, :] = ua
        carry_ref[:, cols] = ua[tm - SUBLANES:, :]
        a = (cb_ref[:, cols] + cw_ref[0:1, cols] * stage_ref[slot, SUBLANES - 2:SUBLANES - 2 + tm, :]
             + cw_ref[1:2, cols] * stage_ref[slot, SUBLANES - 1:SUBLANES - 1 + tm, :] + cw_ref[2:3, cols] * ua)
        act_ref[:, cols] = (_gelu_tanh(a) * uv).astype(BF16)

    out_ref[...] = _rms(out_ref[...] + _dot(act_ref[...], wd_ref[...]), nf_ref[...])


def _ffn_call(x2, og, of, gate, wg, wf, wo, n2, wu, cw, cb, wd, nf, batch, seq):
    tm = ROW_TILE
    n = seq // tm
    rows = lambda b, i: (b * n + i, 0)
    return pl.pallas_call(
        _ffn_kernel,
        grid=(batch, n),
        in_specs=[
            pl.BlockSpec((tm, D_MODEL), rows),
            pl.BlockSpec((tm, GLA_WIDTH), rows),
            pl.BlockSpec((tm, FOX_WIDTH), rows),
            pl.BlockSpec((tm, 2 * D_MODEL), rows),
            _resident((GLA_WIDTH, D_MODEL)),
            _resident((FOX_WIDTH, D_MODEL)),
            _resident((D_MODEL, D_MODEL)),
            _resident((1, D_MODEL)),
            _resident((D_MODEL, 2 * D_FF)),
            _resident((SUBLANES, D_FF)),
            _resident((1, D_FF)),
            _resident((D_FF, D_MODEL)),
            _resident((1, D_MODEL)),
        ],
        out_specs=pl.BlockSpec((tm, D_MODEL), rows),
        out_shape=jax.ShapeDtypeStruct((batch * seq, D_MODEL), F32),
        scratch_shapes=[
            pltpu.VMEM((SUBLANES, D_FF), F32),
            pltpu.VMEM((2, tm + SUBLANES, FF_CHUNK), F32),
            pltpu.VMEM((tm, D_FF), BF16),
        ],
        compiler_params=_GRID_PARAMS,
        name="ffn",
    )(x2, og, of, gate, wg, wf, wo, n2, wu, cw, cb, wd, nf)


def _layer(x2, batch, seq, norm_mix_g, w_in, gla_alpha_w2, gla_alpha_b, gla_out_norm_g,
           fox_forget_b, fox_q_norm_g, fox_k_norm_g, gate_b, w_gla_branch, w_fox_branch,
           w_out, norm_ffn_g, w_up, conv_w, conv_b, w_down, final_g):
    offs = [0]
    for s in IN_SPLITS:
        offs.append(offs[-1] + s)
    sec = lambda a, b: w_in[:, offs[a]:offs[b]]
    small_pad = LANES - LR_COPIES * GLA_LOWRANK - N_SPLIT * FOX_HEADS
    w_small = jnp.concatenate([sec(4, 5)] * LR_COPIES + [sec(8, 9)] * N_SPLIT
                              + [jnp.zeros((D_MODEL, small_pad), w_in.dtype)], axis=1)
    w_packed = jnp.concatenate([
        sec(0, 4), w_small, sec(5, 8), sec(9, 11),
    ], axis=1).astype(BF16)
    w2_hi, w2_mid, w2_lo = _split3(gla_alpha_w2)
    a2 = jnp.concatenate([w2_hi, w2_mid, w2_hi, w2_lo, w2_mid, w2_hi,
                          jnp.zeros((LANES - OFF_FF_IN_SMALL, GLA_KW), BF16)], axis=0)
    fb = jnp.concatenate([jnp.zeros((OFF_FF_IN_SMALL,), F32)] + [fox_forget_b] * N_SPLIT
                         + [jnp.zeros((small_pad,), F32)]).reshape(1, LANES)
    qg = (jnp.tile(fox_q_norm_g, FOX_HEADS) * (FOX_HD ** -0.5 * LOG2E)).reshape(1, FOX_WIDTH)
    kg = jnp.tile(fox_k_norm_g, FOX_HEADS).reshape(1, FOX_WIDTH)
    src = jnp.arange(LANES)[:, None] - OFF_FF_IN_SMALL
    col = jnp.arange(FOX_SPREAD)[None, :]
    place = ((src >= 0) & (src < N_SPLIT * FOX_HEADS)
             & (col == (src % FOX_HEADS) * LANES + FOX_HD + src // FOX_HEADS)).astype(BF16)

    gla_in, la, fq, fk, vt, ct, gates = _proj_call(
        x2, norm_mix_g.reshape(1, D_MODEL), w_packed, a2, gla_alpha_b.reshape(1, GLA_KW), fb, qg, kg,
        gate_b.reshape(1, 2 * D_MODEL), place, batch, seq)

    o_gla = _gla_call(gla_in, la, gla_out_norm_g.reshape(1, GLA_WIDTH), batch, seq)

    o_fox = _fox_call(fq, fk, vt, ct.reshape(batch, FOX_HEADS, seq // ROW_TILE, ROW_TILE), batch, seq)

    cw = jnp.pad(conv_w, ((0, SUBLANES - CONV_W), (0, 0)))
    return _ffn_call(
        x2, o_gla, o_fox, gates, w_gla_branch.astype(BF16), w_fox_branch.astype(BF16), w_out.astype(BF16),
        norm_ffn_g.reshape(1, D_MODEL), w_up.astype(BF16), cw, conv_b.reshape(1, D_FF), w_down.astype(BF16),
        final_g.reshape(1, D_MODEL), batch, seq)


def kernel(x, norm_mix_g, w_in, gla_alpha_w2, gla_alpha_b, gla_out_norm_g, fox_forget_b, fox_q_norm_g,
           fox_k_norm_g, gate_b, w_gla_branch, w_fox_branch, w_out, norm_ffn_g, w_up, conv_w, conv_b,
           w_down, norm_final_g):
    batch, seq, _ = x.shape
    assert norm_mix_g.shape[0] == 1, "the final RMSNorm is fused into the single layer's last kernel"
    assert seq % ROW_TILE == 0
    out = _layer(x.reshape(batch * seq, D_MODEL), batch, seq, norm_mix_g[0], w_in[0], gla_alpha_w2[0],
                 gla_alpha_b[0], gla_out_norm_g[0], fox_forget_b[0], fox_q_norm_g[0], fox_k_norm_g[0],
                 gate_b[0], w_gla_branch[0], w_fox_branch[0], w_out[0], norm_ffn_g[0], w_up[0],
                 conv_w[0], conv_b[0], w_down[0], norm_final_g)
    return out.reshape(batch, seq, D_MODEL)
```

```python
import jax
import jax.numpy as jnp
from jax import lax
from jax.experimental import pallas as pl
from jax.experimental.pallas import tpu as pltpu

D_MODEL = 1024
CHUNK = 64
GLA_HEADS = 4
GLA_DK = 64
GLA_DV = 128
GLA_LOWRANK = 16
GLA_TAU = 16.0
FOX_HEADS = 8
FOX_HD = 64
D_FF = 2816
CONV_W = 3
EPS = 1e-6

GLA_KW = GLA_HEADS * GLA_DK
GLA_WIDTH = GLA_HEADS * GLA_DV
FOX_WIDTH = FOX_HEADS * FOX_HD
IN_SPLITS = (GLA_KW, GLA_KW, GLA_WIDTH, GLA_WIDTH, GLA_LOWRANK,
             FOX_WIDTH, FOX_WIDTH, FOX_WIDTH, FOX_HEADS, D_MODEL, D_MODEL)

LANES = 128
SUBLANES = 8
BF16_ROWS = 16
MXU_N = 256
VMEM_LIMIT = 56 * 1024 * 1024

N_SPLIT = 3
LOG2E = 1.4426950408889634

SEC_GLA = 2 * GLA_KW + 2 * GLA_WIDTH
FOX_SPREAD = FOX_HEADS * LANES
LR_COPIES = 6
OFF_FF_IN_SMALL = LR_COPIES * GLA_LOWRANK

ROW_TILE = 512
FF_CHUNK = MXU_N
N_FF_CHUNKS = D_FF // FF_CHUNK
FOX_GROUP = 4
VT_ROWS = FOX_HD + BF16_ROWS

BF16 = jnp.bfloat16
F32 = jnp.float32
NEG = -1e30


def _dot(a, b):
    return jnp.dot(a, b, preferred_element_type=F32)


def _dot_nt(a, b):
    return lax.dot_general(a, b, (((1,), (1,)), ((), ())), preferred_element_type=F32)


def _dot_tn(a, b):
    return lax.dot_general(a, b, (((0,), (0,)), ((), ())), preferred_element_type=F32)


def _rms(x, g):
    return x * lax.rsqrt(jnp.mean(x * x, axis=-1, keepdims=True) + EPS) * g


def _log_sigmoid(z):
    return jnp.minimum(z, 0.0) - jnp.log1p(jnp.exp(-jnp.abs(z)))


def _split3(a):
    hi = a.astype(BF16)
    r1 = a - hi.astype(F32)
    mid = r1.astype(BF16)
    lo = (r1 - mid.astype(F32)).astype(BF16)
    return hi, mid, lo


def _cumsum_rows_mxu(tri, a):
    hi, mid, lo = _split3(a)
    return _dot(tri, hi) + _dot(tri, mid) + _dot(tri, lo)


def _cumsum_rows_scan(x):
    n, w = x.shape
    row = lax.broadcasted_iota(jnp.int32, (n, w), 0)
    d = 1
    while d < n:
        if d < SUBLANES:
            shifted = jnp.where(row >= d, pltpu.roll(x, d, axis=0), 0.0)
        else:
            shifted = jnp.concatenate([jnp.zeros((d, w), F32), x[:n - d]], axis=0)
        x = x + shifted
        d *= 2
    return x


def _lower_tri(n):
    r = lax.broadcasted_iota(jnp.int32, (n, n), 0)
    c = lax.broadcasted_iota(jnp.int32, (n, n), 1)
    return r >= c


def _resident(shape):
    return pl.BlockSpec(shape, lambda b, i: (0,) * len(shape), pipeline_mode=pl.Buffered(1))


_GRID_PARAMS = pltpu.CompilerParams(
    dimension_semantics=("arbitrary", "arbitrary"), vmem_limit_bytes=VMEM_LIMIT)


def _proj_kernel(x_ref, g_ref, w_gla_ref, w_small_ref, w_fox_ref, w_gate_ref,
                 a2_ref, ab_ref, fb_ref, qg_ref, kg_ref, gb_ref, place_ref,
                 gla_ref, la_ref, fq_ref, fk_ref, vt_ref, ct_ref, gate_ref, carry_ref):
    tm = x_ref.shape[0]

    @pl.when(pl.program_id(1) == 0)
    def _():
        carry_ref[...] = jnp.zeros_like(carry_ref)

    h = _rms(x_ref[...], g_ref[...]).astype(BF16)

    small = _dot(h, w_small_ref[...])
    gate_ref[...] = jax.nn.sigmoid(_dot(h, w_gate_ref[...]) + gb_ref[...]).astype(BF16)
    lane = lax.broadcasted_iota(jnp.int32, (1, LANES), 1)

    s_hi, s_mid, s_lo = _split3(small)
    grp = lane // GLA_LOWRANK
    glr_terms = jnp.where((grp == 2) | (grp == 4), s_mid, jnp.where(grp == 5, s_lo, s_hi))
    z = _dot(glr_terms, a2_ref[...]) + ab_ref[...]
    la_ref[...] = _log_sigmoid(z) * (1.0 / GLA_TAU)

    lf = _log_sigmoid(small + fb_ref[...])
    c = _cumsum_rows_scan(lf) + carry_ref[...]
    carry_ref[...] = c[tm - 1:tm, :]
    ct_ref[0, 0] = jnp.transpose(c)[OFF_FF_IN_SMALL:OFF_FF_IN_SMALL + FOX_HEADS, :]
    c_hi, c_mid, c_lo = _split3(c * (-LOG2E))
    c_terms = jnp.where(lane < OFF_FF_IN_SMALL + FOX_HEADS, c_hi,
                        jnp.where(lane < OFF_FF_IN_SMALL + 2 * FOX_HEADS, c_mid, c_lo))
    c_place = _dot(c_terms, place_ref[...])

    ones = jnp.where((lane >= FOX_HD) & (lane < FOX_HD + N_SPLIT), 1.0, 0.0)
    low = lane < FOX_HD

    def pair_norm(t, g):
        t2 = t * t
        ms_a = jnp.sum(jnp.where(low, t2, 0.0), axis=-1, keepdims=True) * (1.0 / FOX_HD)
        ms_b = jnp.sum(jnp.where(low, 0.0, t2), axis=-1, keepdims=True) * (1.0 / FOX_HD)
        return t * jnp.where(low, lax.rsqrt(ms_a + EPS), lax.rsqrt(ms_b + EPS)) * g

    fqk = _dot(h, w_fox_ref[:, 0:2 * FOX_WIDTH])
    for pr in range(FOX_HEADS // 2):
        qn = pair_norm(fqk[:, pr * LANES:(pr + 1) * LANES], qg_ref[:, pr * LANES:(pr + 1) * LANES])
        kn = pair_norm(fqk[:, FOX_WIDTH + pr * LANES:FOX_WIDTH + (pr + 1) * LANES],
                       kg_ref[:, pr * LANES:(pr + 1) * LANES])
        for half, (qh, kh) in enumerate(((qn, kn), (pltpu.roll(qn, FOX_HD, axis=1), pltpu.roll(kn, FOX_HD, axis=1)))):
            sl = slice((2 * pr + half) * LANES, (2 * pr + half + 1) * LANES)
            fq_ref[:, sl] = jnp.where(low, qh, ones).astype(BF16)
            fk_ref[:, sl] = jnp.where(low, kh, c_place[:, sl]).astype(BF16)

    fvt = jnp.transpose(_dot(h, w_fox_ref[:, 2 * FOX_WIDTH:])).astype(BF16)
    for hd in range(FOX_HEADS):
        vt_ref[0, 0, hd * VT_ROWS:hd * VT_ROWS + FOX_HD, :] = fvt[hd * FOX_HD:(hd + 1) * FOX_HD, :]
        vt_ref[0, 0, hd * VT_ROWS + FOX_HD:(hd + 1) * VT_ROWS, :] = jnp.ones((BF16_ROWS, tm), BF16)

    gla_ref[...] = _dot(h, w_gla_ref[...]).astype(BF16)


def _proj_call(x2, g, w_gla, w_small, w_fox, w_gate, a2, ab, fb, qg, kg, gb, place, batch, seq):
    tm = ROW_TILE
    n = seq // tm
    rows = lambda b, i: (b * n + i, 0)
    tokens = batch * seq
    return pl.pallas_call(
        _proj_kernel,
        grid=(batch, n),
        in_specs=[
            pl.BlockSpec((tm, D_MODEL), rows),
            _resident((1, D_MODEL)),
            _resident((D_MODEL, SEC_GLA)),
            _resident((D_MODEL, LANES)),
            _resident((D_MODEL, 3 * FOX_WIDTH)),
            _resident((D_MODEL, 2 * D_MODEL)),
            _resident((LANES, GLA_KW)),
            _resident((1, GLA_KW)),
            _resident((1, LANES)),
            _resident((1, FOX_WIDTH)),
            _resident((1, FOX_WIDTH)),
            _resident((1, 2 * D_MODEL)),
            _resident((LANES, FOX_SPREAD)),
        ],
        out_specs=[
            pl.BlockSpec((tm, SEC_GLA), rows),
            pl.BlockSpec((tm, GLA_KW), rows),
            pl.BlockSpec((tm, FOX_SPREAD), rows),
            pl.BlockSpec((tm, FOX_SPREAD), rows),
            pl.BlockSpec((1, 1, FOX_HEADS * VT_ROWS, tm), lambda b, i: (b, i, 0, 0)),
            pl.BlockSpec((1, 1, FOX_HEADS, tm), lambda b, i: (b, i, 0, 0)),
            pl.BlockSpec((tm, 2 * D_MODEL), rows),
        ],
        out_shape=[
            jax.ShapeDtypeStruct((tokens, SEC_GLA), BF16),
            jax.ShapeDtypeStruct((tokens, GLA_KW), F32),
            jax.ShapeDtypeStruct((tokens, FOX_SPREAD), BF16),
            jax.ShapeDtypeStruct((tokens, FOX_SPREAD), BF16),
            jax.ShapeDtypeStruct((batch, n, FOX_HEADS * VT_ROWS, tm), BF16),
            jax.ShapeDtypeStruct((batch, n, FOX_HEADS, tm), F32),
            jax.ShapeDtypeStruct((tokens, 2 * D_MODEL), BF16),
        ],
        scratch_shapes=[pltpu.VMEM((1, LANES), F32)],
        compiler_params=_GRID_PARAMS,
        name="proj",
    )(x2, g, w_gla, w_small, w_fox, w_gate, a2, ab, fb, qg, kg, gb, place)


def _gla_kernel(gin_ref, la_ref, ng_ref, o_ref, st_ref):
    tm = gin_ref.shape[0]
    half = MXU_N
    chunks = range(tm // CHUNK)
    heads = range(GLA_HEADS)

    @pl.when(pl.program_id(1) == 0)
    def _():
        st_ref[...] = jnp.zeros_like(st_ref)

    causal = _lower_tri(CHUNK)
    r_i = lax.broadcasted_iota(jnp.int32, (half, half), 0)
    c_i = lax.broadcasted_iota(jnp.int32, (half, half), 1)
    chunk_tri = ((r_i // CHUNK == c_i // CHUNK) & (r_i >= c_i)).astype(BF16)
    lane_head = lax.broadcasted_iota(jnp.int32, (1, GLA_KW), 1) // GLA_DK
    ng = ng_ref[...]
    rows = [slice(c * CHUNK, (c + 1) * CHUNK) for c in chunks]
    vcol = [slice(2 * GLA_KW + hd * GLA_DV, 2 * GLA_KW + (hd + 1) * GLA_DV) for hd in heads]

    b_halves = [_cumsum_rows_mxu(chunk_tri, la_ref[hf * half:(hf + 1) * half, :]) for hf in range(tm // half)]
    per_half = half // CHUNK
    bs = [b_halves[c // per_half][(c % per_half) * CHUNK:(c % per_half + 1) * CHUNK, :] for c in chunks]

    q_e, k_e, k_d, dec = [], [], [], []
    for c in chunks:
        b_last = bs[c][CHUNK - 1:CHUNK, :]
        q = gin_ref[rows[c], 0:GLA_KW].astype(F32) * (GLA_DK ** -0.5)
        k = gin_ref[rows[c], GLA_KW:2 * GLA_KW].astype(F32)
        q_e.append((q * jnp.exp(bs[c])).astype(BF16))
        k_e.append((k * jnp.exp(-bs[c])).astype(BF16))
        k_d.append((k * jnp.exp(b_last - bs[c])).astype(BF16))
        dec.append(jnp.transpose(jnp.broadcast_to(jnp.exp(b_last), (LANES, GLA_KW))))

    q_h = [[jnp.where(lane_head == hd, q_e[c], jnp.zeros_like(q_e[c])) for hd in heads] for c in chunks]
    s = [[jnp.where(causal, _dot_nt(q_h[c][hd], k_e[c]), 0.0).astype(BF16) for hd in heads] for c in chunks]
    o_intra = [[_dot(s[c][hd], gin_ref[rows[c], vcol[hd]]) for hd in heads] for c in chunks]

    kv = []
    for c in chunks:
        full = _dot_tn(k_d[c], gin_ref[rows[c], 2 * GLA_KW:2 * GLA_KW + GLA_WIDTH])
        kv.append(jnp.concatenate(
            [full[hd * GLA_DK:(hd + 1) * GLA_DK, hd * GLA_DV:(hd + 1) * GLA_DV] for hd in heads], axis=0))

    state = st_ref[...]
    states = []
    for c in chunks:
        states.append(state.astype(BF16))
        state = dec[c] * state + kv[c]
    st_ref[...] = state

    for c in chunks:
        r = gin_ref[rows[c], 2 * GLA_KW + GLA_WIDTH:].astype(F32)
        outs = [_rms(o_intra[c][hd] + _dot(q_h[c][hd], states[c]), ng[:, hd * GLA_DV:(hd + 1) * GLA_DV])
                for hd in heads]
        o_ref[rows[c], :] = (jnp.concatenate(outs, axis=1) * (r * jax.nn.sigmoid(r))).astype(o_ref.dtype)


def _gla_call(gin, la, ng, batch, seq):
    tm = ROW_TILE
    n = seq // tm
    rows = lambda b, i: (b * n + i, 0)
    return pl.pallas_call(
        _gla_kernel,
        grid=(batch, n),
        in_specs=[
            pl.BlockSpec((tm, SEC_GLA), rows),
            pl.BlockSpec((tm, GLA_KW), rows),
            pl.BlockSpec((1, GLA_WIDTH), lambda b, i: (0, 0)),
        ],
        out_specs=pl.BlockSpec((tm, GLA_WIDTH), rows),
        out_shape=jax.ShapeDtypeStruct((batch * seq, GLA_WIDTH), BF16),
        scratch_shapes=[pltpu.VMEM((GLA_KW, GLA_DV), F32)],
        compiler_params=_GRID_PARAMS,
        name="gla",
    )(gin, la, ng)


def _fox_kernel(q_ref, k_ref, vt_ref, ct_ref, o_ref, s_ref, p_ref, acc_ref):
    grp = pl.program_id(1)
    seq = q_ref.shape[0]
    blk = vt_ref.shape[3]
    nq = seq // blk
    heads = range(FOX_GROUP)
    lanes = [slice(j * LANES, (j + 1) * LANES) for j in heads]

    def scores(qi, ki):
        q0 = pl.multiple_of(qi * blk, blk)
        k0 = pl.multiple_of(ki * blk, blk)
        col_max = []
        for j in heads:
            s = _dot_nt(k_ref[pl.ds(k0, blk), lanes[j]], q_ref[pl.ds(q0, blk), lanes[j]])
            s_ref[j] = s
            col_max.append(jnp.max(s, axis=0, keepdims=True))
        return col_max

    def probs(cqs, ms, col_max):
        new_ms, alphas = [], []
        for j in heads:
            m_new = jnp.maximum(ms[j], col_max[j] + cqs[j])
            p_ref[j] = jnp.exp2(s_ref[j] - (m_new - cqs[j])).astype(BF16)
            alphas.append(jnp.exp2(ms[j] - m_new))
            new_ms.append(m_new)
        return new_ms, alphas

    def accumulate(ki, alphas):
        for j in heads:
            acc_ref[j] = alphas[j] * acc_ref[j] + _dot(vt_ref[0, ki, j * VT_ROWS:(j + 1) * VT_ROWS, :], p_ref[j])

    hb = blk // 2
    lo, hi = slice(0, hb), slice(hb, blk)

    def diagonal_probs(cqs, ms):
        tri = (lax.broadcasted_iota(jnp.int32, (hb, hb), 0)
               <= lax.broadcasted_iota(jnp.int32, (hb, hb), 1))
        alphas = []
        for j in heads:
            s00 = jnp.where(tri, s_ref[j, lo, lo], NEG)
            s01 = s_ref[j, lo, hi]
            s11 = jnp.where(tri, s_ref[j, hi, hi], NEG)
            cm = jnp.concatenate(
                [jnp.max(s00, axis=0, keepdims=True),
                 jnp.maximum(jnp.max(s01, axis=0, keepdims=True), jnp.max(s11, axis=0, keepdims=True))], axis=1)
            m_new = jnp.maximum(ms[j], cm + cqs[j])
            ref = m_new - cqs[j]
            p_ref[j, lo, lo] = jnp.exp2(s00 - ref[:, lo]).astype(BF16)
            p_ref[j, lo, hi] = jnp.exp2(s01 - ref[:, hi]).astype(BF16)
            p_ref[j, hi, hi] = jnp.exp2(s11 - ref[:, hi]).astype(BF16)
            alphas.append(jnp.exp2(ms[j] - m_new))
        return alphas

    def diagonal_accumulate(ki, alphas):
        for j in heads:
            vt = vt_ref[0, ki, j * VT_ROWS:(j + 1) * VT_ROWS, :]
            acc_ref[j, :, lo] = alphas[j][:, lo] * acc_ref[j, :, lo] + _dot(vt[:, lo], p_ref[j, lo, lo])
            acc_ref[j, :, hi] = alphas[j][:, hi] * acc_ref[j, :, hi] + _dot(vt, p_ref[j, :, hi])

    def q_body(qi, col_max):
        cqs = [ct_ref[0, qi, pl.ds(FOX_GROUP * grp + j, 1), :] * LOG2E for j in heads]
        acc_ref[...] = jnp.zeros_like(acc_ref)

        def kv_step(ki, state):
            ms, col_max = state
            ms, alphas = probs(cqs, ms, col_max)
            col_max = scores(qi, ki + 1)
            accumulate(ki, alphas)
            return ms, col_max

        ms, _ = lax.fori_loop(0, qi, kv_step, ([jnp.full((1, blk), NEG, F32) for _ in heads], col_max))
        alphas = diagonal_probs(cqs, ms)
        col_max = scores(jnp.minimum(qi + 1, nq - 1), 0)
        diagonal_accumulate(qi, alphas)
        o_t = jnp.concatenate([acc_ref[j, 0:FOX_HD, :] / acc_ref[j, FOX_HD:FOX_HD + 1, :] for j in heads], axis=0)
        o_ref[pl.ds(pl.multiple_of(qi * blk, blk), blk), :] = jnp.transpose(o_t).astype(o_ref.dtype)
        return col_max

    lax.fori_loop(0, nq, q_body, scores(0, 0))


def _fox_call(fq, fk, vt, ct4, batch, seq):
    blk = ROW_TILE
    groups = FOX_HEADS // FOX_GROUP
    return pl.pallas_call(
        _fox_kernel,
        grid=(batch, groups),
        in_specs=[
            pl.BlockSpec((seq, FOX_GROUP * LANES), lambda b, p: (b, p)),
            pl.BlockSpec((seq, FOX_GROUP * LANES), lambda b, p: (b, p)),
            pl.BlockSpec((1, seq // blk, FOX_GROUP * VT_ROWS, blk), lambda b, p: (b, 0, p, 0)),
            pl.BlockSpec((1, seq // blk, FOX_HEADS, blk), lambda b, p: (b, 0, 0, 0)),
        ],
        out_specs=pl.BlockSpec((seq, FOX_GROUP * FOX_HD), lambda b, p: (b, p)),
        out_shape=jax.ShapeDtypeStruct((batch * seq, FOX_WIDTH), BF16),
        scratch_shapes=[
            pltpu.VMEM((FOX_GROUP, blk, blk), F32),
            pltpu.VMEM((FOX_GROUP, blk, blk), BF16),
            pltpu.VMEM((FOX_GROUP, VT_ROWS, blk), F32),
        ],
        compiler_params=_GRID_PARAMS,
        name="fox",
    )(fq, fk, vt, ct4)


def _gelu_tanh(a):
    return 0.5 * a * (1.0 + jnp.tanh(0.7978845608028654 * (a + 0.044715 * (a * a * a))))


def _ffn_kernel(x_ref, og_ref, of_ref, gate_ref, wg_ref, wf_ref, wo_ref, n2_ref,
                wu_ref, cw_ref, cb_ref, wd_ref, nf_ref, out_ref,
                carry_ref, stage_ref, act_ref):
    tm = x_ref.shape[0]

    @pl.when(pl.program_id(1) == 0)
    def _():
        carry_ref[...] = jnp.zeros_like(carry_ref)

    y = (gate_ref[:, 0:D_MODEL].astype(F32) * _dot(og_ref[...], wg_ref[...])
         + gate_ref[:, D_MODEL:].astype(F32) * _dot(of_ref[...], wf_ref[...]))
    x1 = x_ref[...] + _dot(y.astype(BF16), wo_ref[...])
    out_ref[...] = x1
    h2 = _rms(x1, n2_ref[...]).astype(BF16)

    for j in range(N_FF_CHUNKS):
        cols = slice(j * FF_CHUNK, (j + 1) * FF_CHUNK)
        ua = _dot(h2, wu_ref[:, cols])
        uv = _dot(h2, wu_ref[:, D_FF + j * FF_CHUNK:D_FF + (j + 1) * FF_CHUNK])
        slot = j % stage_ref.shape[0]
        stage_ref[slot, 0:SUBLANES, :] = carry_ref[:, cols]
        stage_ref[slot, SUBLANES:, :] = ua
        carry_ref[:, cols] = ua[tm - SUBLANES:, :]
        a = (cb_ref[:, cols] + cw_ref[0:1, cols] * stage_ref[slot, SUBLANES - 2:SUBLANES - 2 + tm, :]
             + cw_ref[1:2, cols] * stage_ref[slot, SUBLANES - 1:SUBLANES - 1 + tm, :] + cw_ref[2:3, cols] * ua)
        act_ref[:, cols] = (_gelu_tanh(a) * uv).astype(BF16)

    out_ref[...] = _rms(out_ref[...] + _dot(act_ref[...], wd_ref[...]), nf_ref[...])


def _ffn_call(x2, og, of, gate, wg, wf, wo, n2, wu, cw, cb, wd, nf, batch, seq):
    tm = ROW_TILE
    n = seq // tm
    rows = lambda b, i: (b * n + i, 0)
    return pl.pallas_call(
        _ffn_kernel,
        grid=(batch, n),
        in_specs=[
            pl.BlockSpec((tm, D_MODEL), rows),
            pl.BlockSpec((tm, GLA_WIDTH), rows),
            pl.BlockSpec((tm, FOX_WIDTH), rows),
            pl.BlockSpec((tm, 2 * D_MODEL), rows),
            _resident((GLA_WIDTH, D_MODEL)),
            _resident((FOX_WIDTH, D_MODEL)),
            _resident((D_MODEL, D_MODEL)),
            _resident((1, D_MODEL)),
            _resident((D_MODEL, 2 * D_FF)),
            _resident((SUBLANES, D_FF)),
            _resident((1, D_FF)),
            _resident((D_FF, D_MODEL)),
            _resident((1, D_MODEL)),
        ],
        out_specs=pl.BlockSpec((tm, D_MODEL), rows),
        out_shape=jax.ShapeDtypeStruct((batch * seq, D_MODEL), F32),
        scratch_shapes=[
            pltpu.VMEM((SUBLANES, D_FF), F32),
            pltpu.VMEM((2, tm + SUBLANES, FF_CHUNK), F32),
            pltpu.VMEM((tm, D_FF), BF16),
        ],
        compiler_params=_GRID_PARAMS,
        name="ffn",
    )(x2, og, of, gate, wg, wf, wo, n2, wu, cw, cb, wd, nf)


def _layer(x2, batch, seq, norm_mix_g, w_in, gla_alpha_w2, gla_alpha_b, gla_out_norm_g,
           fox_forget_b, fox_q_norm_g, fox_k_norm_g, gate_b, w_gla_branch, w_fox_branch,
           w_out, norm_ffn_g, w_up, conv_w, conv_b, w_down, final_g):
    offs = [0]
    for s in IN_SPLITS:
        offs.append(offs[-1] + s)
    sec = lambda a, b: w_in[:, offs[a]:offs[b]]
    small_pad = LANES - LR_COPIES * GLA_LOWRANK - N_SPLIT * FOX_HEADS
    w_small = jnp.concatenate([sec(4, 5)] * LR_COPIES + [sec(8, 9)] * N_SPLIT
                              + [jnp.zeros((D_MODEL, small_pad), w_in.dtype)], axis=1).astype(BF16)
    w_gla, w_fox, w_gate = sec(0, 4).astype(BF16), sec(5, 8).astype(BF16), sec(9, 11).astype(BF16)
    w2_hi, w2_mid, w2_lo = _split3(gla_alpha_w2)
    a2 = jnp.concatenate([w2_hi, w2_mid, w2_hi, w2_lo, w2_mid, w2_hi,
                          jnp.zeros((LANES - OFF_FF_IN_SMALL, GLA_KW), BF16)], axis=0)
    fb = jnp.concatenate([jnp.zeros((OFF_FF_IN_SMALL,), F32)] + [fox_forget_b] * N_SPLIT
                         + [jnp.zeros((small_pad,), F32)]).reshape(1, LANES)
    qg = (jnp.tile(fox_q_norm_g, FOX_HEADS) * (FOX_HD ** -0.5 * LOG2E)).reshape(1, FOX_WIDTH)
    kg = jnp.tile(fox_k_norm_g, FOX_HEADS).reshape(1, FOX_WIDTH)
    src = jnp.arange(LANES)[:, None] - OFF_FF_IN_SMALL
    col = jnp.arange(FOX_SPREAD)[None, :]
    place = ((src >= 0) & (src < N_SPLIT * FOX_HEADS)
             & (col == (src % FOX_HEADS) * LANES + FOX_HD + src // FOX_HEADS)).astype(BF16)

    gla_in, la, fq, fk, vt, ct, gates = _proj_call(
        x2, norm_mix_g.reshape(1, D_MODEL), w_gla, w_small, w_fox, w_gate, a2, gla_alpha_b.reshape(1, GLA_KW),
        fb, qg, kg, gate_b.reshape(1, 2 * D_MODEL), place, batch, seq)

    o_gla = _gla_call(gla_in, la, gla_out_norm_g.reshape(1, GLA_WIDTH), batch, seq)

    o_fox = _fox_call(fq, fk, vt, ct, batch, seq)

    cw = jnp.pad(conv_w, ((0, SUBLANES - CONV_W), (0, 0)))
    return _ffn_call(
        x2, o_gla, o_fox, gates, w_gla_branch.astype(BF16), w_fox_branch.astype(BF16), w_out.astype(BF16),
        norm_ffn_g.reshape(1, D_MODEL), w_up.astype(BF16), cw, conv_b.reshape(1, D_FF), w_down.astype(BF16),
        final_g.reshape(1, D_MODEL), batch, seq)


def kernel(x, norm_mix_g, w_in, gla_alpha_w2, gla_alpha_b, gla_out_norm_g, fox_forget_b, fox_q_norm_g,
           fox_k_norm_g, gate_b, w_gla_branch, w_fox_branch, w_out, norm_ffn_g, w_up, conv_w, conv_b,
           w_down, norm_final_g):
    batch, seq, _ = x.shape
    assert norm_mix_g.shape[0] == 1, "the final RMSNorm is fused into the single layer's last kernel"
    assert seq % ROW_TILE == 0
    out = _layer(x.reshape(batch * seq, D_MODEL), batch, seq, norm_mix_g[0], w_in[0], gla_alpha_w2[0],
                 gla_alpha_b[0], gla_out_norm_g[0], fox_forget_b[0], fox_q_norm_g[0], fox_k_norm_g[0],
                 gate_b[0], w_gla_branch[0], w_fox_branch[0], w_out[0], norm_ffn_g[0], w_up[0],
                 conv_w[0], conv_b[0], w_down[0], norm_final_g)
    return out.reshape(batch, seq, D_MODEL)
```

```python
import jax
import jax.numpy as jnp
from jax import lax
from jax.experimental import pallas as pl
from jax.experimental.pallas import tpu as pltpu

D_MODEL = 1024
CHUNK = 64
GLA_HEADS = 4
GLA_DK = 64
GLA_DV = 128
GLA_LOWRANK = 16
GLA_TAU = 16.0
FOX_HEADS = 8
FOX_HD = 64
D_FF = 2816
CONV_W = 3
EPS = 1e-6

GLA_KW = GLA_HEADS * GLA_DK
GLA_WIDTH = GLA_HEADS * GLA_DV
FOX_WIDTH = FOX_HEADS * FOX_HD
IN_SPLITS = (GLA_KW, GLA_KW, GLA_WIDTH, GLA_WIDTH, GLA_LOWRANK,
             FOX_WIDTH, FOX_WIDTH, FOX_WIDTH, FOX_HEADS, D_MODEL, D_MODEL)

LANES = 128
SUBLANES = 8
BF16_ROWS = 16
MXU_N = 256
VMEM_LIMIT = 56 * 1024 * 1024

N_SPLIT = 3
LOG2E = 1.4426950408889634

SEC_GLA = 2 * GLA_KW + 2 * GLA_WIDTH
FOX_SPREAD = FOX_HEADS * LANES
LR_COPIES = 6
OFF_FF_IN_SMALL = LR_COPIES * GLA_LOWRANK

ROW_TILE = 512
FF_CHUNK = MXU_N
N_FF_CHUNKS = D_FF // FF_CHUNK
FOX_GROUP = 4
VT_ROWS = FOX_HD + BF16_ROWS

BF16 = jnp.bfloat16
F32 = jnp.float32
NEG = -1e30


def _dot(a, b):
    return jnp.dot(a, b, preferred_element_type=F32)


def _dot_nt(a, b):
    return lax.dot_general(a, b, (((1,), (1,)), ((), ())), preferred_element_type=F32)


def _rms(x, g):
    return x * lax.rsqrt(jnp.mean(x * x, axis=-1, keepdims=True) + EPS) * g


def _log_sigmoid(z):
    return jnp.minimum(z, 0.0) - jnp.log1p(jnp.exp(-jnp.abs(z)))


def _split3(a):
    hi = a.astype(BF16)
    r1 = a - hi.astype(F32)
    mid = r1.astype(BF16)
    lo = (r1 - mid.astype(F32)).astype(BF16)
    return hi, mid, lo


def _cumsum_rows_scan(x, period=None):
    n, w = x.shape
    period = period or n
    row = lax.broadcasted_iota(jnp.int32, (n, w), 0) & (period - 1)
    d = 1
    while d < period:
        if d < SUBLANES:
            shifted = jnp.where(row >= d, pltpu.roll(x, d, axis=0), 0.0)
        else:
            shifted = jnp.concatenate([jnp.zeros((d, w), F32), x[:n - d]], axis=0)
            if period < n:
                shifted = jnp.where(row >= d, shifted, 0.0)
        x = x + shifted
        d *= 2
    return x


def _lower_tri(n):
    r = lax.broadcasted_iota(jnp.int32, (n, n), 0)
    c = lax.broadcasted_iota(jnp.int32, (n, n), 1)
    return r >= c


def _resident(shape):
    return pl.BlockSpec(shape, lambda b, i: (0,) * len(shape), pipeline_mode=pl.Buffered(1))


_GRID_PARAMS = pltpu.CompilerParams(
    dimension_semantics=("arbitrary", "arbitrary"), vmem_limit_bytes=VMEM_LIMIT)


def _proj_kernel(x_ref, g_ref, w_gla_ref, w_small_ref, w_fox_ref, w_gate_ref,
                 a2_ref, ab_ref, fb_ref, qg_ref, kg_ref, gb_ref, place_ref,
                 gla_ref, la_ref, fq_ref, fk_ref, vt_ref, ct_ref, gate_ref, carry_ref):
    tm = x_ref.shape[0]

    @pl.when(pl.program_id(1) == 0)
    def _():
        carry_ref[...] = jnp.zeros_like(carry_ref)

    h = _rms(x_ref[...], g_ref[...]).astype(BF16)

    small = _dot(h, w_small_ref[...])
    gate_ref[...] = jax.nn.sigmoid(_dot(h, w_gate_ref[...]) + gb_ref[...]).astype(BF16)
    lane = lax.broadcasted_iota(jnp.int32, (1, LANES), 1)

    s_hi, s_mid, s_lo = _split3(small)
    grp = lane // GLA_LOWRANK
    glr_terms = jnp.where((grp == 2) | (grp == 4), s_mid, jnp.where(grp == 5, s_lo, s_hi))
    z = _dot(glr_terms, a2_ref[...]) + ab_ref[...]
    la_ref[...] = _log_sigmoid(z) * (1.0 / GLA_TAU)

    lf = _log_sigmoid(small + fb_ref[...])
    c = _cumsum_rows_scan(lf) + carry_ref[...]
    carry_ref[...] = c[tm - 1:tm, :]
    ct_ref[0, 0] = jnp.transpose(c)[OFF_FF_IN_SMALL:OFF_FF_IN_SMALL + FOX_HEADS, :]
    c_hi, c_mid, c_lo = _split3(c * (-LOG2E))
    c_terms = jnp.where(lane < OFF_FF_IN_SMALL + FOX_HEADS, c_hi,
                        jnp.where(lane < OFF_FF_IN_SMALL + 2 * FOX_HEADS, c_mid, c_lo))
    c_place = _dot(c_terms, place_ref[...])

    ones = jnp.where((lane >= FOX_HD) & (lane < FOX_HD + N_SPLIT), 1.0, 0.0)
    low = lane < FOX_HD

    def pair_norm(t, g):
        t2 = t * t
        ms_a = jnp.sum(jnp.where(low, t2, 0.0), axis=-1, keepdims=True) * (1.0 / FOX_HD)
        ms_b = jnp.sum(jnp.where(low, 0.0, t2), axis=-1, keepdims=True) * (1.0 / FOX_HD)
        return t * jnp.where(low, lax.rsqrt(ms_a + EPS), lax.rsqrt(ms_b + EPS)) * g

    fqk = _dot(h, w_fox_ref[:, 0:2 * FOX_WIDTH])
    for pr in range(FOX_HEADS // 2):
        qn = pair_norm(fqk[:, pr * LANES:(pr + 1) * LANES], qg_ref[:, pr * LANES:(pr + 1) * LANES])
        kn = pair_norm(fqk[:, FOX_WIDTH + pr * LANES:FOX_WIDTH + (pr + 1) * LANES],
                       kg_ref[:, pr * LANES:(pr + 1) * LANES])
        for half, (qh, kh) in enumerate(((qn, kn), (pltpu.roll(qn, FOX_HD, axis=1), pltpu.roll(kn, FOX_HD, axis=1)))):
            sl = slice((2 * pr + half) * LANES, (2 * pr + half + 1) * LANES)
            fq_ref[:, sl] = jnp.where(low, qh, ones).astype(BF16)
            fk_ref[:, sl] = jnp.where(low, kh, c_place[:, sl]).astype(BF16)

    fvt = jnp.transpose(_dot(h, w_fox_ref[:, 2 * FOX_WIDTH:])).astype(BF16)
    for hd in range(FOX_HEADS):
        vt_ref[0, 0, hd * VT_ROWS:hd * VT_ROWS + FOX_HD, :] = fvt[hd * FOX_HD:(hd + 1) * FOX_HD, :]
        vt_ref[0, 0, hd * VT_ROWS + FOX_HD:(hd + 1) * VT_ROWS, :] = jnp.ones((BF16_ROWS, tm), BF16)

    gla_ref[...] = _dot(h, w_gla_ref[...]).astype(BF16)


def _proj_call(x2, g, w_gla, w_small, w_fox, w_gate, a2, ab, fb, qg, kg, gb, place, batch, seq):
    tm = ROW_TILE
    n = seq // tm
    rows = lambda b, i: (b * n + i, 0)
    tokens = batch * seq
    return pl.pallas_call(
        _proj_kernel,
        grid=(batch, n),
        in_specs=[
            pl.BlockSpec((tm, D_MODEL), rows),
            _resident((1, D_MODEL)),
            _resident((D_MODEL, SEC_GLA)),
            _resident((D_MODEL, LANES)),
            _resident((D_MODEL, 3 * FOX_WIDTH)),
            _resident((D_MODEL, 2 * D_MODEL)),
            _resident((LANES, GLA_KW)),
            _resident((1, GLA_KW)),
            _resident((1, LANES)),
            _resident((1, FOX_WIDTH)),
            _resident((1, FOX_WIDTH)),
            _resident((1, 2 * D_MODEL)),
            _resident((LANES, FOX_SPREAD)),
        ],
        out_specs=[
            pl.BlockSpec((tm, SEC_GLA), rows),
            pl.BlockSpec((tm, GLA_KW), rows),
            pl.BlockSpec((tm, FOX_SPREAD), rows),
            pl.BlockSpec((tm, FOX_SPREAD), rows),
            pl.BlockSpec((1, 1, FOX_HEADS * VT_ROWS, tm), lambda b, i: (b, i, 0, 0)),
            pl.BlockSpec((1, 1, FOX_HEADS, tm), lambda b, i: (b, i, 0, 0)),
            pl.BlockSpec((tm, 2 * D_MODEL), rows),
        ],
        out_shape=[
            jax.ShapeDtypeStruct((tokens, SEC_GLA), BF16),
            jax.ShapeDtypeStruct((tokens, GLA_KW), F32),
            jax.ShapeDtypeStruct((tokens, FOX_SPREAD), BF16),
            jax.ShapeDtypeStruct((tokens, FOX_SPREAD), BF16),
            jax.ShapeDtypeStruct((batch, n, FOX_HEADS * VT_ROWS, tm), BF16),
            jax.ShapeDtypeStruct((batch, n, FOX_HEADS, tm), F32),
            jax.ShapeDtypeStruct((tokens, 2 * D_MODEL), BF16),
        ],
        scratch_shapes=[pltpu.VMEM((1, LANES), F32)],
        compiler_params=_GRID_PARAMS,
        name="proj",
    )(x2, g, w_gla, w_small, w_fox, w_gate, a2, ab, fb, qg, kg, gb, place)


def _gla_kernel(gin_ref, la_ref, ng_ref, o_ref, st_ref):
    tm = gin_ref.shape[0]
    chunks = range(tm // CHUNK)
    heads = range(GLA_HEADS)

    @pl.when(pl.program_id(1) == 0)
    def _():
        st_ref[...] = jnp.zeros_like(st_ref)

    causal = _lower_tri(CHUNK)
    lane_head = lax.broadcasted_iota(jnp.int32, (1, GLA_KW), 1) // GLA_DK
    ng = ng_ref[...]
    rows = [slice(c * CHUNK, (c + 1) * CHUNK) for c in chunks]
    pair_rows = [slice((c // 2) * 2 * CHUNK, (c // 2 + 1) * 2 * CHUNK) for c in chunks]
    vcol = [slice(2 * GLA_KW + hd * GLA_DV, 2 * GLA_KW + (hd + 1) * GLA_DV) for hd in heads]

    b_all = _cumsum_rows_scan(la_ref[...], period=CHUNK)
    bs = [b_all[rows[c], :] for c in chunks]

    q_e, k_e, k_d, dec = [], [], [], []
    for c in chunks:
        b_last = bs[c][CHUNK - 1:CHUNK, :]
        q = gin_ref[rows[c], 0:GLA_KW].astype(F32) * (GLA_DK ** -0.5)
        k = gin_ref[rows[c], GLA_KW:2 * GLA_KW].astype(F32)
        q_e.append((q * jnp.exp(bs[c])).astype(BF16))
        k_e.append((k * jnp.exp(-bs[c])).astype(BF16))
        k_d.append(k * jnp.exp(b_last - bs[c]))
        dec.append(jnp.transpose(jnp.broadcast_to(jnp.exp(b_last), (LANES, GLA_KW))))

    q_h = [[jnp.where(lane_head == hd, q_e[c], jnp.zeros_like(q_e[c])) for hd in heads] for c in chunks]
    s = [[jnp.where(causal, _dot_nt(q_h[c][hd], k_e[c]), 0.0).astype(BF16) for hd in heads] for c in chunks]
    o_intra = [[_dot(s[c][hd], gin_ref[rows[c], vcol[hd]]) for hd in heads] for c in chunks]

    token_lane = lax.broadcasted_iota(jnp.int32, (1, 2 * CHUNK), 1) // CHUNK
    k_dt = []
    for c in chunks[::2]:
        both = jnp.transpose(jnp.concatenate([k_d[c], k_d[c + 1]], axis=0))
        k_dt += [jnp.where(token_lane == half, both, 0.0).astype(BF16) for half in range(2)]
    kv = [jnp.concatenate([_dot(k_dt[c][hd * GLA_DK:(hd + 1) * GLA_DK, :], gin_ref[pair_rows[c], vcol[hd]])
                           for hd in heads], axis=0) for c in chunks]

    state = st_ref[...]
    states = []
    for c in chunks:
        states.append(state.astype(BF16))
        state = dec[c] * state + kv[c]
    st_ref[...] = state

    for c in chunks:
        r = gin_ref[rows[c], 2 * GLA_KW + GLA_WIDTH:].astype(F32)
        outs = [_rms(o_intra[c][hd] + _dot(q_h[c][hd], states[c]), ng[:, hd * GLA_DV:(hd + 1) * GLA_DV])
                for hd in heads]
        o_ref[rows[c], :] = (jnp.concatenate(outs, axis=1) * (r * jax.nn.sigmoid(r))).astype(o_ref.dtype)


def _gla_call(gin, la, ng, batch, seq):
    tm = ROW_TILE
    n = seq // tm
    rows = lambda b, i: (b * n + i, 0)
    return pl.pallas_call(
        _gla_kernel,
        grid=(batch, n),
        in_specs=[
            pl.BlockSpec((tm, SEC_GLA), rows),
            pl.BlockSpec((tm, GLA_KW), rows),
            pl.BlockSpec((1, GLA_WIDTH), lambda b, i: (0, 0)),
        ],
        out_specs=pl.BlockSpec((tm, GLA_WIDTH), rows),
        out_shape=jax.ShapeDtypeStruct((batch * seq, GLA_WIDTH), BF16),
        scratch_shapes=[pltpu.VMEM((GLA_KW, GLA_DV), F32)],
        compiler_params=_GRID_PARAMS,
        name="gla",
    )(gin, la, ng)


def _fox_kernel(q_ref, k_ref, vt_ref, ct_ref, o_ref, s_ref, p_ref, acc_ref):
    grp = pl.program_id(1)
    seq = q_ref.shape[0]
    blk = vt_ref.shape[3]
    nq = seq // blk
    heads = range(FOX_GROUP)
    lanes = [slice(j * LANES, (j + 1) * LANES) for j in heads]

    def scores(qi, ki):
        q0 = pl.multiple_of(qi * blk, blk)
        k0 = pl.multiple_of(ki * blk, blk)
        col_max = []
        for j in heads:
            s = _dot_nt(k_ref[pl.ds(k0, blk), lanes[j]], q_ref[pl.ds(q0, blk), lanes[j]])
            s_ref[j] = s
            col_max.append(jnp.max(s, axis=0, keepdims=True))
        return col_max

    def probs(cqs, ms, col_max):
        new_ms, alphas = [], []
        for j in heads:
            m_new = jnp.maximum(ms[j], col_max[j] + cqs[j])
            p_ref[j] = jnp.exp2(s_ref[j] - (m_new - cqs[j])).astype(BF16)
            alphas.append(jnp.exp2(ms[j] - m_new))
            new_ms.append(m_new)
        return new_ms, alphas

    def accumulate(ki, alphas):
        for j in heads:
            acc_ref[j] = alphas[j] * acc_ref[j] + _dot(vt_ref[0, ki, j * VT_ROWS:(j + 1) * VT_ROWS, :], p_ref[j])

    hb = blk // 2
    lo, hi = slice(0, hb), slice(hb, blk)

    def diagonal_probs(cqs, ms):
        tri = (lax.broadcasted_iota(jnp.int32, (hb, hb), 0)
               <= lax.broadcasted_iota(jnp.int32, (hb, hb), 1))
        alphas = []
        for j in heads:
            s00 = jnp.where(tri, s_ref[j, lo, lo], NEG)
            s01 = s_ref[j, lo, hi]
            s11 = jnp.where(tri, s_ref[j, hi, hi], NEG)
            cm = jnp.concatenate(
                [jnp.max(s00, axis=0, keepdims=True),
                 jnp.maximum(jnp.max(s01, axis=0, keepdims=True), jnp.max(s11, axis=0, keepdims=True))], axis=1)
            m_new = jnp.maximum(ms[j], cm + cqs[j])
            ref = m_new - cqs[j]
            p_ref[j, lo, lo] = jnp.exp2(s00 - ref[:, lo]).astype(BF16)
            p_ref[j, lo, hi] = jnp.exp2(s01 - ref[:, hi]).astype(BF16)
            p_ref[j, hi, hi] = jnp.exp2(s11 - ref[:, hi]).astype(BF16)
            alphas.append(jnp.exp2(ms[j] - m_new))
        return alphas

    def diagonal_accumulate(ki, alphas):
        for j in heads:
            vt = vt_ref[0, ki, j * VT_ROWS:(j + 1) * VT_ROWS, :]
            acc_ref[j, :, lo] = alphas[j][:, lo] * acc_ref[j, :, lo] + _dot(vt[:, lo], p_ref[j, lo, lo])
            acc_ref[j, :, hi] = alphas[j][:, hi] * acc_ref[j, :, hi] + _dot(vt, p_ref[j, :, hi])

    def q_body(qi, col_max):
        cqs = [ct_ref[0, qi, pl.ds(FOX_GROUP * grp + j, 1), :] * LOG2E for j in heads]
        acc_ref[...] = jnp.zeros_like(acc_ref)

        def kv_step(ki, state):
            ms, col_max = state
            ms, alphas = probs(cqs, ms, col_max)
            col_max = scores(qi, ki + 1)
            accumulate(ki, alphas)
            return ms, col_max

        ms, _ = lax.fori_loop(0, qi, kv_step, ([jnp.full((1, blk), NEG, F32) for _ in heads], col_max))
        alphas = diagonal_probs(cqs, ms)
        col_max = scores(jnp.minimum(qi + 1, nq - 1), 0)
        diagonal_accumulate(qi, alphas)
        o_t = jnp.concatenate([acc_ref[j, 0:FOX_HD, :] / acc_ref[j, FOX_HD:FOX_HD + 1, :] for j in heads], axis=0)
        o_ref[pl.ds(pl.multiple_of(qi * blk, blk), blk), :] = jnp.transpose(o_t).astype(o_ref.dtype)
        return col_max

    lax.fori_loop(0, nq, q_body, scores(0, 0))


def _fox_call(fq, fk, vt, ct4, batch, seq):
    blk = ROW_TILE
    groups = FOX_HEADS // FOX_GROUP
    return pl.pallas_call(
        _fox_kernel,
        grid=(batch, groups),
        in_specs=[
            pl.BlockSpec((seq, FOX_GROUP * LANES), lambda b, p: (b, p)),
            pl.BlockSpec((seq, FOX_GROUP * LANES), lambda b, p: (b, p)),
            pl.BlockSpec((1, seq // blk, FOX_GROUP * VT_ROWS, blk), lambda b, p: (b, 0, p, 0)),
            pl.BlockSpec((1, seq // blk, FOX_HEADS, blk), lambda b, p: (b, 0, 0, 0)),
        ],
        out_specs=pl.BlockSpec((seq, FOX_GROUP * FOX_HD), lambda b, p: (b, p)),
        out_shape=jax.ShapeDtypeStruct((batch * seq, FOX_WIDTH), BF16),
        scratch_shapes=[
            pltpu.VMEM((FOX_GROUP, blk, blk), F32),
            pltpu.VMEM((FOX_GROUP, blk, blk), BF16),
            pltpu.VMEM((FOX_GROUP, VT_ROWS, blk), F32),
        ],
        compiler_params=_GRID_PARAMS,
        name="fox",
    )(fq, fk, vt, ct4)


def _gelu_tanh(a):
    return 0.5 * a * (1.0 + jnp.tanh(0.7978845608028654 * (a + 0.044715 * (a * a * a))))


def _ffn_kernel(x_ref, og_ref, of_ref, gate_ref, wg_ref, wf_ref, wo_ref, n2_ref,
                wu_ref, cw_ref, cb_ref, wd_ref, nf_ref, out_ref,
                carry_ref, stage_ref, act_ref):
    tm = x_ref.shape[0]

    @pl.when(pl.program_id(1) == 0)
    def _():
        carry_ref[...] = jnp.zeros_like(carry_ref)

    y = (gate_ref[:, 0:D_MODEL].astype(F32) * _dot(og_ref[...], wg_ref[...])
         + gate_ref[:, D_MODEL:].astype(F32) * _dot(of_ref[...], wf_ref[...]))
    x1 = x_ref[...] + _dot(y.astype(BF16), wo_ref[...])
    out_ref[...] = x1
    h2 = _rms(x1, n2_ref[...]).astype(BF16)

    for j in range(N_FF_CHUNKS):
        cols = slice(j * FF_CHUNK, (j + 1) * FF_CHUNK)
        ua = _dot(h2, wu_ref[:, cols])
        uv = _dot(h2, wu_ref[:, D_FF + j * FF_CHUNK:D_FF + (j + 1) * FF_CHUNK])
        slot = j % stage_ref.shape[0]
        stage_ref[slot, 0:SUBLANES, :] = carry_ref[:, cols]
        stage_ref[slot, SUBLANES:, :] = ua
        carry_ref[:, cols] = ua[tm - SUBLANES:, :]
        a = (cb_ref[:, cols] + cw_ref[0:1, cols] * stage_ref[slot, SUBLANES - 2:SUBLANES - 2 + tm, :]
             + cw_ref[1:2, cols] * stage_ref[slot, SUBLANES - 1:SUBLANES - 1 + tm, :] + cw_ref[2:3, cols] * ua)
        act_ref[:, cols] = (_gelu_tanh(a) * uv).astype(BF16)

    out_ref[...] = _rms(out_ref[...] + _dot(act_ref[...], wd_ref[...]), nf_ref[...])


def _ffn_call(x2, og, of, gate, wg, wf, wo, n2, wu, cw, cb, wd, nf, batch, seq):
    tm = ROW_TILE
    n = seq // tm
    rows = lambda b, i: (b * n + i, 0)
    return pl.pallas_call(
        _ffn_kernel,
        grid=(batch, n),
        in_specs=[
            pl.BlockSpec((tm, D_MODEL), rows),
            pl.BlockSpec((tm, GLA_WIDTH), rows),
            pl.BlockSpec((tm, FOX_WIDTH), rows),
            pl.BlockSpec((tm, 2 * D_MODEL), rows),
            _resident((GLA_WIDTH, D_MODEL)),
            _resident((FOX_WIDTH, D_MODEL)),
            _resident((D_MODEL, D_MODEL)),
            _resident((1, D_MODEL)),
            _resident((D_MODEL, 2 * D_FF)),
            _resident((SUBLANES, D_FF)),
            _resident((1, D_FF)),
            _resident((D_FF, D_MODEL)),
            _resident((1, D_MODEL)),
        ],
        out_specs=pl.BlockSpec((tm, D_MODEL), rows),
        out_shape=jax.ShapeDtypeStruct((batch * seq, D_MODEL), F32),
        scratch_shapes=[
            pltpu.VMEM((SUBLANES, D_FF), F32),
            pltpu.VMEM((2, tm + SUBLANES, FF_CHUNK), F32),
            pltpu.VMEM((tm, D_FF), BF16),
        ],
        compiler_params=_GRID_PARAMS,
        name="ffn",
    )(x2, og, of, gate, wg, wf, wo, n2, wu, cw, cb, wd, nf)


def _layer(x2, batch, seq, norm_mix_g, w_in, gla_alpha_w2, gla_alpha_b, gla_out_norm_g,
           fox_forget_b, fox_q_norm_g, fox_k_norm_g, gate_b, w_gla_branch, w_fox_branch,
           w_out, norm_ffn_g, w_up, conv_w, conv_b, w_down, final_g):
    offs = [0]
    for s in IN_SPLITS:
        offs.append(offs[-1] + s)
    sec = lambda a, b: w_in[:, offs[a]:offs[b]]
    small_pad = LANES - LR_COPIES * GLA_LOWRANK - N_SPLIT * FOX_HEADS
    w_small = jnp.concatenate([sec(4, 5)] * LR_COPIES + [sec(8, 9)] * N_SPLIT
                              + [jnp.zeros((D_MODEL, small_pad), w_in.dtype)], axis=1).astype(BF16)
    w_gla, w_fox, w_gate = sec(0, 4).astype(BF16), sec(5, 8).astype(BF16), sec(9, 11).astype(BF16)
    w2_hi, w2_mid, w2_lo = _split3(gla_alpha_w2)
    a2 = jnp.concatenate([w2_hi, w2_mid, w2_hi, w2_lo, w2_mid, w2_hi,
                          jnp.zeros((LANES - OFF_FF_IN_SMALL, GLA_KW), BF16)], axis=0)
    fb = jnp.concatenate([jnp.zeros((OFF_FF_IN_SMALL,), F32)] + [fox_forget_b] * N_SPLIT
                         + [jnp.zeros((small_pad,), F32)]).reshape(1, LANES)
    qg = (jnp.tile(fox_q_norm_g, FOX_HEADS) * (FOX_HD ** -0.5 * LOG2E)).reshape(1, FOX_WIDTH)
    kg = jnp.tile(fox_k_norm_g, FOX_HEADS).reshape(1, FOX_WIDTH)
    src = jnp.arange(LANES)[:, None] - OFF_FF_IN_SMALL
    col = jnp.arange(FOX_SPREAD)[None, :]
    place = ((src >= 0) & (src < N_SPLIT * FOX_HEADS)
             & (col == (src % FOX_HEADS) * LANES + FOX_HD + src // FOX_HEADS)).astype(BF16)

    gla_in, la, fq, fk, vt, ct, gates = _proj_call(
        x2, norm_mix_g.reshape(1, D_MODEL), w_gla, w_small, w_fox, w_gate, a2, gla_alpha_b.reshape(1, GLA_KW),
        fb, qg, kg, gate_b.reshape(1, 2 * D_MODEL), place, batch, seq)

    o_gla = _gla_call(gla_in, la, gla_out_norm_g.reshape(1, GLA_WIDTH), batch, seq)

    o_fox = _fox_call(fq, fk, vt, ct, batch, seq)

    cw = jnp.pad(conv_w, ((0, SUBLANES - CONV_W), (0, 0)))
    return _ffn_call(
        x2, o_gla, o_fox, gates, w_gla_branch.astype(BF16), w_fox_branch.astype(BF16), w_out.astype(BF16),
        norm_ffn_g.reshape(1, D_MODEL), w_up.astype(BF16), cw, conv_b.reshape(1, D_FF), w_down.astype(BF16),
        final_g.reshape(1, D_MODEL), batch, seq)


def kernel(x, norm_mix_g, w_in, gla_alpha_w2, gla_alpha_b, gla_out_norm_g, fox_forget_b, fox_q_norm_g,
           fox_k_norm_g, gate_b, w_gla_branch, w_fox_branch, w_out, norm_ffn_g, w_up, conv_w, conv_b,
           w_down, norm_final_g):
    batch, seq, _ = x.shape
    assert norm_mix_g.shape[0] == 1, "the final RMSNorm is fused into the single layer's last kernel"
    assert seq % ROW_TILE == 0
    out = _layer(x.reshape(batch * seq, D_MODEL), batch, seq, norm_mix_g[0], w_in[0], gla_alpha_w2[0],
                 gla_alpha_b[0], gla_out_norm_g[0], fox_forget_b[0], fox_q_norm_g[0], fox_k_norm_g[0],
                 gate_b[0], w_gla_branch[0], w_fox_branch[0], w_out[0], norm_ffn_g[0], w_up[0],
                 conv_w[0], conv_b[0], w_down[0], norm_final_g)
    return out.reshape(batch, seq, D_MODEL)
```

```python
import jax
import jax.numpy as jnp
from jax import lax
from jax.experimental import pallas as pl
from jax.experimental.pallas import tpu as pltpu

D_MODEL = 1024
CHUNK = 64
GLA_HEADS = 4
GLA_DK = 64
GLA_DV = 128
GLA_LOWRANK = 16
GLA_TAU = 16.0
FOX_HEADS = 8
FOX_HD = 64
D_FF = 2816
CONV_W = 3
EPS = 1e-6

GLA_KW = GLA_HEADS * GLA_DK
GLA_WIDTH = GLA_HEADS * GLA_DV
FOX_WIDTH = FOX_HEADS * FOX_HD
IN_SPLITS = (GLA_KW, GLA_KW, GLA_WIDTH, GLA_WIDTH, GLA_LOWRANK,
             FOX_WIDTH, FOX_WIDTH, FOX_WIDTH, FOX_HEADS, D_MODEL, D_MODEL)

LANES = 128
SUBLANES = 8
BF16_ROWS = 16
MXU_N = 256
VMEM_LIMIT = 56 * 1024 * 1024

N_SPLIT = 3
LOG2E = 1.4426950408889634

SEC_GLA = 2 * GLA_KW + 2 * GLA_WIDTH
FOX_SPREAD = FOX_HEADS * LANES
LR_COPIES = 6
OFF_FF_IN_SMALL = LR_COPIES * GLA_LOWRANK

ROW_TILE = 512
FF_CHUNK = MXU_N
N_FF_CHUNKS = D_FF // FF_CHUNK
FOX_GROUP = 4
VT_ROWS = FOX_HD + BF16_ROWS

BF16 = jnp.bfloat16
F32 = jnp.float32
NEG = -1e30


def _dot(a, b):
    return jnp.dot(a, b, preferred_element_type=F32)


def _dot_nt(a, b):
    return lax.dot_general(a, b, (((1,), (1,)), ((), ())), preferred_element_type=F32)


def _rms(x, g):
    return x * lax.rsqrt(jnp.mean(x * x, axis=-1, keepdims=True) + EPS) * g


def _log_sigmoid(z):
    return jnp.minimum(z, 0.0) - jnp.log1p(jnp.exp(-jnp.abs(z)))


def _split3(a):
    hi = a.astype(BF16)
    r1 = a - hi.astype(F32)
    mid = r1.astype(BF16)
    lo = (r1 - mid.astype(F32)).astype(BF16)
    return hi, mid, lo


def _cumsum_rows_scan(x, period=None):
    n, w = x.shape
    period = period or n
    row = lax.broadcasted_iota(jnp.int32, (n, w), 0) & (period - 1)
    d = 1
    while d < period:
        if d < SUBLANES:
            shifted = jnp.where(row >= d, pltpu.roll(x, d, axis=0), 0.0)
        else:
            shifted = jnp.concatenate([jnp.zeros((d, w), F32), x[:n - d]], axis=0)
            if period < n:
                shifted = jnp.where(row >= d, shifted, 0.0)
        x = x + shifted
        d *= 2
    return x


def _lower_tri(n):
    r = lax.broadcasted_iota(jnp.int32, (n, n), 0)
    c = lax.broadcasted_iota(jnp.int32, (n, n), 1)
    return r >= c


def _resident(shape):
    return pl.BlockSpec(shape, lambda b, i: (0,) * len(shape), pipeline_mode=pl.Buffered(1))


_GRID_PARAMS = pltpu.CompilerParams(
    dimension_semantics=("arbitrary", "arbitrary"), vmem_limit_bytes=VMEM_LIMIT)


def _proj_kernel(x_ref, g_ref, w_gla_ref, w_small_ref, w_fox_ref, w_gate_ref,
                 a2_ref, ab_ref, fb_ref, qg_ref, kg_ref, gb_ref, place_ref,
                 gla_ref, la_ref, fq_ref, fk_ref, vt_ref, ct_ref, gate_ref, carry_ref):
    tm = x_ref.shape[0]

    @pl.when(pl.program_id(1) == 0)
    def _():
        carry_ref[...] = jnp.zeros_like(carry_ref)

    h = _rms(x_ref[...], g_ref[...]).astype(BF16)

    small = _dot(h, w_small_ref[...])
    gate_ref[...] = jax.nn.sigmoid(_dot(h, w_gate_ref[...]) + gb_ref[...]).astype(BF16)
    lane = lax.broadcasted_iota(jnp.int32, (1, LANES), 1)

    s_hi, s_mid, s_lo = _split3(small)
    grp = lane // GLA_LOWRANK
    glr_terms = jnp.where((grp == 2) | (grp == 4), s_mid, jnp.where(grp == 5, s_lo, s_hi))
    z = _dot(glr_terms, a2_ref[...]) + ab_ref[...]
    la_ref[...] = _log_sigmoid(z) * (1.0 / GLA_TAU)

    lf = _log_sigmoid(small + fb_ref[...])
    c = _cumsum_rows_scan(lf) + carry_ref[...]
    carry_ref[...] = c[tm - 1:tm, :]
    ct_ref[0, 0] = jnp.transpose(c)[OFF_FF_IN_SMALL:OFF_FF_IN_SMALL + FOX_HEADS, :]
    c_hi, c_mid, c_lo = _split3(c * (-LOG2E))
    c_terms = jnp.where(lane < OFF_FF_IN_SMALL + FOX_HEADS, c_hi,
                        jnp.where(lane < OFF_FF_IN_SMALL + 2 * FOX_HEADS, c_mid, c_lo))
    c_place = _dot(c_terms, place_ref[...])

    ones = jnp.where((lane >= FOX_HD) & (lane < FOX_HD + N_SPLIT), 1.0, 0.0)
    low = lane < FOX_HD

    def pair_norm(t, g):
        t2 = t * t
        ms_a = jnp.sum(jnp.where(low, t2, 0.0), axis=-1, keepdims=True) * (1.0 / FOX_HD)
        ms_b = jnp.sum(jnp.where(low, 0.0, t2), axis=-1, keepdims=True) * (1.0 / FOX_HD)
        return t * jnp.where(low, lax.rsqrt(ms_a + EPS), lax.rsqrt(ms_b + EPS)) * g

    fqk = _dot(h, w_fox_ref[:, 0:2 * FOX_WIDTH])
    for pr in range(FOX_HEADS // 2):
        qn = pair_norm(fqk[:, pr * LANES:(pr + 1) * LANES], qg_ref[:, pr * LANES:(pr + 1) * LANES])
        kn = pair_norm(fqk[:, FOX_WIDTH + pr * LANES:FOX_WIDTH + (pr + 1) * LANES],
                       kg_ref[:, pr * LANES:(pr + 1) * LANES])
        for half, (qh, kh) in enumerate(((qn, kn), (pltpu.roll(qn, FOX_HD, axis=1), pltpu.roll(kn, FOX_HD, axis=1)))):
            sl = slice((2 * pr + half) * LANES, (2 * pr + half + 1) * LANES)
            fq_ref[0, 0, sl, :] = jnp.transpose(jnp.where(low, qh, ones)).astype(BF16)
            fk_ref[:, sl] = jnp.where(low, kh, c_place[:, sl]).astype(BF16)

    fvt = jnp.transpose(_dot(h, w_fox_ref[:, 2 * FOX_WIDTH:])).astype(BF16)
    for hd in range(FOX_HEADS):
        vt_ref[0, 0, hd * VT_ROWS:hd * VT_ROWS + FOX_HD, :] = fvt[hd * FOX_HD:(hd + 1) * FOX_HD, :]
        vt_ref[0, 0, hd * VT_ROWS + FOX_HD:(hd + 1) * VT_ROWS, :] = jnp.ones((BF16_ROWS, tm), BF16)

    gla_ref[...] = _dot(h, w_gla_ref[...]).astype(BF16)


def _proj_call(x2, g, w_gla, w_small, w_fox, w_gate, a2, ab, fb, qg, kg, gb, place, batch, seq):
    tm = ROW_TILE
    n = seq // tm
    rows = lambda b, i: (b * n + i, 0)
    tokens = batch * seq
    return pl.pallas_call(
        _proj_kernel,
        grid=(batch, n),
        in_specs=[
            pl.BlockSpec((tm, D_MODEL), rows),
            _resident((1, D_MODEL)),
            _resident((D_MODEL, SEC_GLA)),
            _resident((D_MODEL, LANES)),
            _resident((D_MODEL, 3 * FOX_WIDTH)),
            _resident((D_MODEL, 2 * D_MODEL)),
            _resident((LANES, GLA_KW)),
            _resident((1, GLA_KW)),
            _resident((1, LANES)),
            _resident((1, FOX_WIDTH)),
            _resident((1, FOX_WIDTH)),
            _resident((1, 2 * D_MODEL)),
            _resident((LANES, FOX_SPREAD)),
        ],
        out_specs=[
            pl.BlockSpec((tm, SEC_GLA), rows),
            pl.BlockSpec((tm, GLA_KW), rows),
            pl.BlockSpec((1, 1, FOX_SPREAD, tm), lambda b, i: (b, i, 0, 0)),
            pl.BlockSpec((tm, FOX_SPREAD), rows),
            pl.BlockSpec((1, 1, FOX_HEADS * VT_ROWS, tm), lambda b, i: (b, i, 0, 0)),
            pl.BlockSpec((1, 1, FOX_HEADS, tm), lambda b, i: (b, i, 0, 0)),
            pl.BlockSpec((tm, 2 * D_MODEL), rows),
        ],
        out_shape=[
            jax.ShapeDtypeStruct((tokens, SEC_GLA), BF16),
            jax.ShapeDtypeStruct((tokens, GLA_KW), F32),
            jax.ShapeDtypeStruct((batch, n, FOX_SPREAD, tm), BF16),
            jax.ShapeDtypeStruct((tokens, FOX_SPREAD), BF16),
            jax.ShapeDtypeStruct((batch, n, FOX_HEADS * VT_ROWS, tm), BF16),
            jax.ShapeDtypeStruct((batch, n, FOX_HEADS, tm), F32),
            jax.ShapeDtypeStruct((tokens, 2 * D_MODEL), BF16),
        ],
        scratch_shapes=[pltpu.VMEM((1, LANES), F32)],
        compiler_params=_GRID_PARAMS,
        name="proj",
    )(x2, g, w_gla, w_small, w_fox, w_gate, a2, ab, fb, qg, kg, gb, place)


def _gla_kernel(gin_ref, la_ref, ng_ref, o_ref, st_ref):
    tm = gin_ref.shape[0]
    chunks = range(tm // CHUNK)
    heads = range(GLA_HEADS)

    @pl.when(pl.program_id(1) == 0)
    def _():
        st_ref[...] = jnp.zeros_like(st_ref)

    causal = _lower_tri(CHUNK)
    lane_head = lax.broadcasted_iota(jnp.int32, (1, GLA_KW), 1) // GLA_DK
    ng = ng_ref[...]
    rows = [slice(c * CHUNK, (c + 1) * CHUNK) for c in chunks]
    pair_rows = [slice((c // 2) * 2 * CHUNK, (c // 2 + 1) * 2 * CHUNK) for c in chunks]
    vcol = [slice(2 * GLA_KW + hd * GLA_DV, 2 * GLA_KW + (hd + 1) * GLA_DV) for hd in heads]

    b_all = _cumsum_rows_scan(la_ref[...], period=CHUNK)
    bs = [b_all[rows[c], :] for c in chunks]

    q_e, k_e, k_d, dec = [], [], [], []
    for c in chunks:
        b_last = bs[c][CHUNK - 1:CHUNK, :]
        q = gin_ref[rows[c], 0:GLA_KW].astype(F32) * (GLA_DK ** -0.5)
        k = gin_ref[rows[c], GLA_KW:2 * GLA_KW].astype(F32)
        q_e.append((q * jnp.exp(bs[c])).astype(BF16))
        k_e.append((k * jnp.exp(-bs[c])).astype(BF16))
        k_d.append(k * jnp.exp(b_last - bs[c]))
        dec.append(jnp.transpose(jnp.broadcast_to(jnp.exp(b_last), (LANES, GLA_KW))))

    q_h = [[jnp.where(lane_head == hd, q_e[c], jnp.zeros_like(q_e[c])) for hd in heads] for c in chunks]
    s = [[jnp.where(causal, _dot_nt(q_h[c][hd], k_e[c]), 0.0).astype(BF16) for hd in heads] for c in chunks]
    o_intra = [[_dot(s[c][hd], gin_ref[rows[c], vcol[hd]]) for hd in heads] for c in chunks]

    token_lane = lax.broadcasted_iota(jnp.int32, (1, 2 * CHUNK), 1) // CHUNK
    k_dt = []
    for c in chunks[::2]:
        both = jnp.transpose(jnp.concatenate([k_d[c], k_d[c + 1]], axis=0))
        k_dt += [jnp.where(token_lane == half, both, 0.0).astype(BF16) for half in range(2)]
    kv = [jnp.concatenate([_dot(k_dt[c][hd * GLA_DK:(hd + 1) * GLA_DK, :], gin_ref[pair_rows[c], vcol[hd]])
                           for hd in heads], axis=0) for c in chunks]

    state = st_ref[...]
    states = []
    for c in chunks:
        states.append(state.astype(BF16))
        state = dec[c] * state + kv[c]
    st_ref[...] = state

    for c in chunks:
        r = gin_ref[rows[c], 2 * GLA_KW + GLA_WIDTH:].astype(F32)
        outs = [_rms(o_intra[c][hd] + _dot(q_h[c][hd], states[c]), ng[:, hd * GLA_DV:(hd + 1) * GLA_DV])
                for hd in heads]
        o_ref[rows[c], :] = (jnp.concatenate(outs, axis=1) * (r * jax.nn.sigmoid(r))).astype(o_ref.dtype)


def _gla_call(gin, la, ng, batch, seq):
    tm = ROW_TILE
    n = seq // tm
    rows = lambda b, i: (b * n + i, 0)
    return pl.pallas_call(
        _gla_kernel,
        grid=(batch, n),
        in_specs=[
            pl.BlockSpec((tm, SEC_GLA), rows),
            pl.BlockSpec((tm, GLA_KW), rows),
            pl.BlockSpec((1, GLA_WIDTH), lambda b, i: (0, 0)),
        ],
        out_specs=pl.BlockSpec((tm, GLA_WIDTH), rows),
        out_shape=jax.ShapeDtypeStruct((batch * seq, GLA_WIDTH), BF16),
        scratch_shapes=[pltpu.VMEM((GLA_KW, GLA_DV), F32)],
        compiler_params=_GRID_PARAMS,
        name="gla",
    )(gin, la, ng)


def _fox_kernel(qt_ref, k_ref, vt_ref, ct_ref, o_ref, s_ref, p_ref, acc_ref):
    grp = pl.program_id(1)
    seq = k_ref.shape[0]
    blk = vt_ref.shape[3]
    nq = seq // blk
    heads = range(FOX_GROUP)
    lanes = [slice(j * LANES, (j + 1) * LANES) for j in heads]

    def scores(qi, ki):
        k0 = pl.multiple_of(ki * blk, blk)
        col_max = []
        for j in heads:
            s = _dot(k_ref[pl.ds(k0, blk), lanes[j]], qt_ref[0, qi, lanes[j], :])
            s_ref[j] = s
            col_max.append(jnp.max(s, axis=0, keepdims=True))
        return col_max

    def probs(cqs, ms, col_max):
        new_ms, alphas = [], []
        for j in heads:
            m_new = jnp.maximum(ms[j], col_max[j] + cqs[j])
            p_ref[j] = jnp.exp2(s_ref[j] - (m_new - cqs[j])).astype(BF16)
            alphas.append(jnp.exp2(ms[j] - m_new))
            new_ms.append(m_new)
        return new_ms, alphas

    def accumulate(ki, alphas):
        for j in heads:
            acc_ref[j] = alphas[j] * acc_ref[j] + _dot(vt_ref[0, ki, j * VT_ROWS:(j + 1) * VT_ROWS, :], p_ref[j])

    hb = blk // 2
    lo, hi = slice(0, hb), slice(hb, blk)

    def diagonal_probs(cqs, ms):
        tri = (lax.broadcasted_iota(jnp.int32, (hb, hb), 0)
               <= lax.broadcasted_iota(jnp.int32, (hb, hb), 1))
        alphas = []
        for j in heads:
            s00 = jnp.where(tri, s_ref[j, lo, lo], NEG)
            s01 = s_ref[j, lo, hi]
            s11 = jnp.where(tri, s_ref[j, hi, hi], NEG)
            cm = jnp.concatenate(
                [jnp.max(s00, axis=0, keepdims=True),
                 jnp.maximum(jnp.max(s01, axis=0, keepdims=True), jnp.max(s11, axis=0, keepdims=True))], axis=1)
            m_new = jnp.maximum(ms[j], cm + cqs[j])
            ref = m_new - cqs[j]
            p_ref[j, lo, lo] = jnp.exp2(s00 - ref[:, lo]).astype(BF16)
            p_ref[j, lo, hi] = jnp.exp2(s01 - ref[:, hi]).astype(BF16)
            p_ref[j, hi, hi] = jnp.exp2(s11 - ref[:, hi]).astype(BF16)
            alphas.append(jnp.exp2(ms[j] - m_new))
        return alphas

    def diagonal_accumulate(ki, alphas):
        for j in heads:
            vt = vt_ref[0, ki, j * VT_ROWS:(j + 1) * VT_ROWS, :]
            acc_ref[j, :, lo] = alphas[j][:, lo] * acc_ref[j, :, lo] + _dot(vt[:, lo], p_ref[j, lo, lo])
            acc_ref[j, :, hi] = alphas[j][:, hi] * acc_ref[j, :, hi] + _dot(vt, p_ref[j, :, hi])

    def q_body(qi, col_max):
        cqs = [ct_ref[0, qi, pl.ds(FOX_GROUP * grp + j, 1), :] * LOG2E for j in heads]
        acc_ref[...] = jnp.zeros_like(acc_ref)

        def kv_step(ki, state):
            ms, col_max = state
            ms, alphas = probs(cqs, ms, col_max)
            col_max = scores(qi, ki + 1)
            accumulate(ki, alphas)
            return ms, col_max

        ms, _ = lax.fori_loop(0, qi, kv_step, ([jnp.full((1, blk), NEG, F32) for _ in heads], col_max))
        alphas = diagonal_probs(cqs, ms)
        col_max = scores(jnp.minimum(qi + 1, nq - 1), 0)
        diagonal_accumulate(qi, alphas)
        o_t = jnp.concatenate([acc_ref[j, 0:FOX_HD, :] / acc_ref[j, FOX_HD:FOX_HD + 1, :] for j in heads], axis=0)
        o_ref[pl.ds(pl.multiple_of(qi * blk, blk), blk), :] = jnp.transpose(o_t).astype(o_ref.dtype)
        return col_max

    lax.fori_loop(0, nq, q_body, scores(0, 0))


def _fox_call(fq, fk, vt, ct4, batch, seq):
    blk = ROW_TILE
    groups = FOX_HEADS // FOX_GROUP
    return pl.pallas_call(
        _fox_kernel,
        grid=(batch, groups),
        in_specs=[
            pl.BlockSpec((1, seq // blk, FOX_GROUP * LANES, blk), lambda b, p: (b, 0, p, 0)),
            pl.BlockSpec((seq, FOX_GROUP * LANES), lambda b, p: (b, p)),
            pl.BlockSpec((1, seq // blk, FOX_GROUP * VT_ROWS, blk), lambda b, p: (b, 0, p, 0)),
            pl.BlockSpec((1, seq // blk, FOX_HEADS, blk), lambda b, p: (b, 0, 0, 0)),
        ],
        out_specs=pl.BlockSpec((seq, FOX_GROUP * FOX_HD), lambda b, p: (b, p)),
        out_shape=jax.ShapeDtypeStruct((batch * seq, FOX_WIDTH), BF16),
        scratch_shapes=[
            pltpu.VMEM((FOX_GROUP, blk, blk), F32),
            pltpu.VMEM((FOX_GROUP, blk, blk), BF16),
            pltpu.VMEM((FOX_GROUP, VT_ROWS, blk), F32),
        ],
        compiler_params=_GRID_PARAMS,
        name="fox",
    )(fq, fk, vt, ct4)


def _gelu_tanh(a):
    return 0.5 * a * (1.0 + jnp.tanh(0.7978845608028654 * (a + 0.044715 * (a * a * a))))


def _ffn_kernel(x_ref, og_ref, of_ref, gate_ref, wg_ref, wf_ref, wo_ref, n2_ref,
                wu_ref, cw_ref, cb_ref, wd_ref, nf_ref, out_ref,
                carry_ref, stage_ref, act_ref):
    tm = x_ref.shape[0]

    @pl.when(pl.program_id(1) == 0)
    def _():
        carry_ref[...] = jnp.zeros_like(carry_ref)

    y = (gate_ref[:, 0:D_MODEL].astype(F32) * _dot(og_ref[...], wg_ref[...])
         + gate_ref[:, D_MODEL:].astype(F32) * _dot(of_ref[...], wf_ref[...]))
    x1 = x_ref[...] + _dot(y.astype(BF16), wo_ref[...])
    out_ref[...] = x1
    h2 = _rms(x1, n2_ref[...]).astype(BF16)

    for j in range(N_FF_CHUNKS):
        cols = slice(j * FF_CHUNK, (j + 1) * FF_CHUNK)
        ua = _dot(h2, wu_ref[:, cols])
        uv = _dot(h2, wu_ref[:, D_FF + j * FF_CHUNK:D_FF + (j + 1) * FF_CHUNK])
        slot = j % stage_ref.shape[0]
        stage_ref[slot, 0:SUBLANES, :] = carry_ref[:, cols]
        stage_ref[slot, SUBLANES:, :] = ua
        carry_ref[:, cols] = ua[tm - SUBLANES:, :]
        a = (cb_ref[:, cols] + cw_ref[0:1, cols] * stage_ref[slot, SUBLANES - 2:SUBLANES - 2 + tm, :]
             + cw_ref[1:2, cols] * stage_ref[slot, SUBLANES - 1:SUBLANES - 1 + tm, :] + cw_ref[2:3, cols] * ua)
        act_ref[:, cols] = (_gelu_tanh(a) * uv).astype(BF16)

    out_ref[...] = _rms(out_ref[...] + _dot(act_ref[...], wd_ref[...]), nf_ref[...])


def _ffn_call(x2, og, of, gate, wg, wf, wo, n2, wu, cw, cb, wd, nf, batch, seq):
    tm = ROW_TILE
    n = seq // tm
    rows = lambda b, i: (b * n + i, 0)
    return pl.pallas_call(
        _ffn_kernel,
        grid=(batch, n),
        in_specs=[
            pl.BlockSpec((tm, D_MODEL), rows),
            pl.BlockSpec((tm, GLA_WIDTH), rows),
            pl.BlockSpec((tm, FOX_WIDTH), rows),
            pl.BlockSpec((tm, 2 * D_MODEL), rows),
            _resident((GLA_WIDTH, D_MODEL)),
            _resident((FOX_WIDTH, D_MODEL)),
            _resident((D_MODEL, D_MODEL)),
            _resident((1, D_MODEL)),
            _resident((D_MODEL, 2 * D_FF)),
            _resident((SUBLANES, D_FF)),
            _resident((1, D_FF)),
            _resident((D_FF, D_MODEL)),
            _resident((1, D_MODEL)),
        ],
        out_specs=pl.BlockSpec((tm, D_MODEL), rows),
        out_shape=jax.ShapeDtypeStruct((batch * seq, D_MODEL), F32),
        scratch_shapes=[
            pltpu.VMEM((SUBLANES, D_FF), F32),
            pltpu.VMEM((2, tm + SUBLANES, FF_CHUNK), F32),
            pltpu.VMEM((tm, D_FF), BF16),
        ],
        compiler_params=_GRID_PARAMS,
        name="ffn",
    )(x2, og, of, gate, wg, wf, wo, n2, wu, cw, cb, wd, nf)


def _layer(x2, batch, seq, norm_mix_g, w_in, gla_alpha_w2, gla_alpha_b, gla_out_norm_g,
           fox_forget_b, fox_q_norm_g, fox_k_norm_g, gate_b, w_gla_branch, w_fox_branch,
           w_out, norm_ffn_g, w_up, conv_w, conv_b, w_down, final_g):
    offs = [0]
    for s in IN_SPLITS:
        offs.append(offs[-1] + s)
    sec = lambda a, b: w_in[:, offs[a]:offs[b]]
    small_pad = LANES - LR_COPIES * GLA_LOWRANK - N_SPLIT * FOX_HEADS
    w_small = jnp.concatenate([sec(4, 5)] * LR_COPIES + [sec(8, 9)] * N_SPLIT
                              + [jnp.zeros((D_MODEL, small_pad), w_in.dtype)], axis=1).astype(BF16)
    w_gla, w_fox, w_gate = sec(0, 4).astype(BF16), sec(5, 8).astype(BF16), sec(9, 11).astype(BF16)
    w2_hi, w2_mid, w2_lo = _split3(gla_alpha_w2)
    a2 = jnp.concatenate([w2_hi, w2_mid, w2_hi, w2_lo, w2_mid, w2_hi,
                          jnp.zeros((LANES - OFF_FF_IN_SMALL, GLA_KW), BF16)], axis=0)
    fb = jnp.concatenate([jnp.zeros((OFF_FF_IN_SMALL,), F32)] + [fox_forget_b] * N_SPLIT
                         + [jnp.zeros((small_pad,), F32)]).reshape(1, LANES)
    qg = (jnp.tile(fox_q_norm_g, FOX_HEADS) * (FOX_HD ** -0.5 * LOG2E)).reshape(1, FOX_WIDTH)
    kg = jnp.tile(fox_k_norm_g, FOX_HEADS).reshape(1, FOX_WIDTH)
    src = jnp.arange(LANES)[:, None] - OFF_FF_IN_SMALL
    col = jnp.arange(FOX_SPREAD)[None, :]
    place = ((src >= 0) & (src < N_SPLIT * FOX_HEADS)
             & (col == (src % FOX_HEADS) * LANES + FOX_HD + src // FOX_HEADS)).astype(BF16)

    gla_in, la, fq, fk, vt, ct, gates = _proj_call(
        x2, norm_mix_g.reshape(1, D_MODEL), w_gla, w_small, w_fox, w_gate, a2, gla_alpha_b.reshape(1, GLA_KW),
        fb, qg, kg, gate_b.reshape(1, 2 * D_MODEL), place, batch, seq)

    o_gla = _gla_call(gla_in, la, gla_out_norm_g.reshape(1, GLA_WIDTH), batch, seq)

    o_fox = _fox_call(fq, fk, vt, ct, batch, seq)

    cw = jnp.pad(conv_w, ((0, SUBLANES - CONV_W), (0, 0)))
    return _ffn_call(
        x2, o_gla, o_fox, gates, w_gla_branch.astype(BF16), w_fox_branch.astype(BF16), w_out.astype(BF16),
        norm_ffn_g.reshape(1, D_MODEL), w_up.astype(BF16), cw, conv_b.reshape(1, D_FF), w_down.astype(BF16),
        final_g.reshape(1, D_MODEL), batch, seq)


def kernel(x, norm_mix_g, w_in, gla_alpha_w2, gla_alpha_b, gla_out_norm_g, fox_forget_b, fox_q_norm_g,
           fox_k_norm_g, gate_b, w_gla_branch, w_fox_branch, w_out, norm_ffn_g, w_up, conv_w, conv_b,
           w_down, norm_final_g):
    batch, seq, _ = x.shape
    assert norm_mix_g.shape[0] == 1, "the final RMSNorm is fused into the single layer's last kernel"
    assert seq % ROW_TILE == 0
    out = _layer(x.reshape(batch * seq, D_MODEL), batch, seq, norm_mix_g[0], w_in[0], gla_alpha_w2[0],
                 gla_alpha_b[0], gla_out_norm_g[0], fox_forget_b[0], fox_q_norm_g[0], fox_k_norm_g[0],
                 gate_b[0], w_gla_branch[0], w_fox_branch[0], w_out[0], norm_ffn_g[0], w_up[0],
                 conv_w[0], conv_b[0], w_down[0], norm_final_g)
    return out.reshape(batch, seq, D_MODEL)
```

```python
import jax
import jax.numpy as jnp
from jax import lax
from jax.experimental import pallas as pl
from jax.experimental.pallas import tpu as pltpu

D_MODEL = 1024
CHUNK = 64
GLA_HEADS = 4
GLA_DK = 64
GLA_DV = 128
GLA_LOWRANK = 16
GLA_TAU = 16.0
FOX_HEADS = 8
FOX_HD = 64
D_FF = 2816
CONV_W = 3
EPS = 1e-6

GLA_KW = GLA_HEADS * GLA_DK
GLA_WIDTH = GLA_HEADS * GLA_DV
FOX_WIDTH = FOX_HEADS * FOX_HD
IN_SPLITS = (GLA_KW, GLA_KW, GLA_WIDTH, GLA_WIDTH, GLA_LOWRANK,
             FOX_WIDTH, FOX_WIDTH, FOX_WIDTH, FOX_HEADS, D_MODEL, D_MODEL)

LANES = 128
SUBLANES = 8
BF16_ROWS = 16
MXU_N = 256
VMEM_LIMIT = 56 * 1024 * 1024

N_SPLIT = 3
LOG2E = 1.4426950408889634

SEC_GLA = 2 * GLA_KW + 2 * GLA_WIDTH
LR_COPIES = 6
OFF_FF_IN_SMALL = LR_COPIES * GLA_LOWRANK
FOX_SPREAD = FOX_HEADS * LANES

ROW_TILE = 512
GLA_TILE = 1024
FF_CHUNK = MXU_N
N_FF_CHUNKS = D_FF // FF_CHUNK
FOX_GROUP = 4
VT_ROWS = FOX_HD + BF16_ROWS

BF16 = jnp.bfloat16
F32 = jnp.float32
NEG = -1e30


def _dot(a, b):
    return jnp.dot(a, b, preferred_element_type=F32)


def _dot_nt(a, b):
    return lax.dot_general(a, b, (((1,), (1,)), ((), ())), preferred_element_type=F32)


def _rms(x, g):
    return x * lax.rsqrt(jnp.mean(x * x, axis=-1, keepdims=True) + EPS) * g


def _log_sigmoid(z):
    return jnp.minimum(z, 0.0) - jnp.log1p(jnp.exp(-jnp.abs(z)))


def _split3(a):
    hi = a.astype(BF16)
    r1 = a - hi.astype(F32)
    mid = r1.astype(BF16)
    lo = (r1 - mid.astype(F32)).astype(BF16)
    return hi, mid, lo


def _cumsum_rows_scan(x, period=None):
    n, w = x.shape
    period = period or n
    row = lax.broadcasted_iota(jnp.int32, (n, w), 0) & (period - 1)
    d = 1
    while d < period:
        if d < SUBLANES:
            shifted = jnp.where(row >= d, pltpu.roll(x, d, axis=0), 0.0)
        else:
            shifted = jnp.concatenate([jnp.zeros((d, w), F32), x[:n - d]], axis=0)
            if period < n:
                shifted = jnp.where(row >= d, shifted, 0.0)
        x = x + shifted
        d *= 2
    return x


def _lower_tri(n):
    r = lax.broadcasted_iota(jnp.int32, (n, n), 0)
    c = lax.broadcasted_iota(jnp.int32, (n, n), 1)
    return r >= c


def _resident(shape):
    return pl.BlockSpec(shape, lambda b, i: (0,) * len(shape), pipeline_mode=pl.Buffered(1))


_GRID_PARAMS = pltpu.CompilerParams(
    dimension_semantics=("arbitrary", "arbitrary"), vmem_limit_bytes=VMEM_LIMIT)


def _proj_kernel(x_ref, g_ref, w_gla_ref, w_small_ref, w_fox_ref, w_gate_ref,
                 a2_ref, ab_ref, fb_ref, qg_ref, kg_ref, gb_ref, place_ref,
                 gla_ref, la_ref, fq_ref, fk_ref, vt_ref, ct_ref, gate_ref, carry_ref):
    tm = x_ref.shape[0]

    @pl.when(pl.program_id(1) == 0)
    def _():
        carry_ref[...] = jnp.zeros_like(carry_ref)

    h = _rms(x_ref[...], g_ref[...]).astype(BF16)

    small = _dot(h, w_small_ref[...])
    gate_ref[...] = jax.nn.sigmoid(_dot(h, w_gate_ref[...]) + gb_ref[...]).astype(BF16)
    lane = lax.broadcasted_iota(jnp.int32, (1, LANES), 1)

    s_hi, s_mid, s_lo = _split3(small)
    grp = lane // GLA_LOWRANK
    glr_terms = jnp.where((grp == 2) | (grp == 4), s_mid, jnp.where(grp == 5, s_lo, s_hi))
    z = _dot(glr_terms, a2_ref[...]) + ab_ref[...]
    la_ref[...] = _log_sigmoid(z) * (1.0 / GLA_TAU)

    lf = _log_sigmoid(small + fb_ref[...])
    c = _cumsum_rows_scan(lf) + carry_ref[...]
    carry_ref[...] = c[tm - 1:tm, :]
    ct_ref[0, 0] = jnp.transpose(c)[OFF_FF_IN_SMALL:OFF_FF_IN_SMALL + FOX_HEADS, :]
    c_hi, c_mid, c_lo = _split3(c * (-LOG2E))
    c_terms = jnp.where(lane < OFF_FF_IN_SMALL + FOX_HEADS, c_hi,
                        jnp.where(lane < OFF_FF_IN_SMALL + 2 * FOX_HEADS, c_mid, c_lo))
    c_place = _dot(c_terms, place_ref[...])

    ones = jnp.where((lane >= FOX_HD) & (lane < FOX_HD + N_SPLIT), 1.0, 0.0)
    low = lane < FOX_HD

    def pair_norm(t, g):
        t2 = t * t
        ms_a = jnp.sum(jnp.where(low, t2, 0.0), axis=-1, keepdims=True) * (1.0 / FOX_HD)
        ms_b = jnp.sum(jnp.where(low, 0.0, t2), axis=-1, keepdims=True) * (1.0 / FOX_HD)
        return t * jnp.where(low, lax.rsqrt(ms_a + EPS), lax.rsqrt(ms_b + EPS)) * g

    fqk = _dot(h, w_fox_ref[:, 0:2 * FOX_WIDTH])
    for pr in range(FOX_HEADS // 2):
        qn = pair_norm(fqk[:, pr * LANES:(pr + 1) * LANES], qg_ref[:, pr * LANES:(pr + 1) * LANES])
        kn = pair_norm(fqk[:, FOX_WIDTH + pr * LANES:FOX_WIDTH + (pr + 1) * LANES],
                       kg_ref[:, pr * LANES:(pr + 1) * LANES])
        for half, (qh, kh) in enumerate(((qn, kn), (pltpu.roll(qn, FOX_HD, axis=1), pltpu.roll(kn, FOX_HD, axis=1)))):
            sl = slice((2 * pr + half) * LANES, (2 * pr + half + 1) * LANES)
            fq_ref[0, 0, sl, :] = jnp.transpose(jnp.where(low, qh, ones)).astype(BF16)
            fk_ref[:, sl] = jnp.where(low, kh, c_place[:, sl]).astype(BF16)

    fvt = jnp.transpose(_dot(h, w_fox_ref[:, 2 * FOX_WIDTH:])).astype(BF16)
    for hd in range(FOX_HEADS):
        vt_ref[0, 0, hd * VT_ROWS:hd * VT_ROWS + FOX_HD, :] = fvt[hd * FOX_HD:(hd + 1) * FOX_HD, :]
        vt_ref[0, 0, hd * VT_ROWS + FOX_HD:(hd + 1) * VT_ROWS, :] = jnp.ones((BF16_ROWS, tm), BF16)

    gla_ref[...] = _dot(h, w_gla_ref[...]).astype(BF16)


def _proj_call(x2, g, w_gla, w_small, w_fox, w_gate, a2, ab, fb, qg, kg, gb, place, batch, seq):
    tm = ROW_TILE
    n = seq // tm
    rows = lambda b, i: (b * n + i, 0)
    tokens = batch * seq
    return pl.pallas_call(
        _proj_kernel,
        grid=(batch, n),
        in_specs=[
            pl.BlockSpec((tm, D_MODEL), rows),
            _resident((1, D_MODEL)),
            _resident((D_MODEL, SEC_GLA)),
            _resident((D_MODEL, LANES)),
            _resident((D_MODEL, 3 * FOX_WIDTH)),
            _resident((D_MODEL, 2 * D_MODEL)),
            _resident((LANES, GLA_KW)),
            _resident((1, GLA_KW)),
            _resident((1, LANES)),
            _resident((1, FOX_WIDTH)),
            _resident((1, FOX_WIDTH)),
            _resident((1, 2 * D_MODEL)),
            _resident((LANES, FOX_SPREAD)),
        ],
        out_specs=[
            pl.BlockSpec((tm, SEC_GLA), rows),
            pl.BlockSpec((tm, GLA_KW), rows),
            pl.BlockSpec((1, 1, FOX_SPREAD, tm), lambda b, i: (b, i, 0, 0)),
            pl.BlockSpec((tm, FOX_SPREAD), rows),
            pl.BlockSpec((1, 1, FOX_HEADS * VT_ROWS, tm), lambda b, i: (b, i, 0, 0)),
            pl.BlockSpec((1, 1, FOX_HEADS, tm), lambda b, i: (b, i, 0, 0)),
            pl.BlockSpec((tm, 2 * D_MODEL), rows),
        ],
        out_shape=[
            jax.ShapeDtypeStruct((tokens, SEC_GLA), BF16),
            jax.ShapeDtypeStruct((tokens, GLA_KW), F32),
            jax.ShapeDtypeStruct((batch, n, FOX_SPREAD, tm), BF16),
            jax.ShapeDtypeStruct((tokens, FOX_SPREAD), BF16),
            jax.ShapeDtypeStruct((batch, n, FOX_HEADS * VT_ROWS, tm), BF16),
            jax.ShapeDtypeStruct((batch, n, FOX_HEADS, tm), F32),
            jax.ShapeDtypeStruct((tokens, 2 * D_MODEL), BF16),
        ],
        scratch_shapes=[pltpu.VMEM((1, LANES), F32)],
        compiler_params=_GRID_PARAMS,
        name="proj",
    )(x2, g, w_gla, w_small, w_fox, w_gate, a2, ab, fb, qg, kg, gb, place)


def _gla_kernel(gin_ref, la_ref, ng_ref, o_ref, st_ref):
    tm = gin_ref.shape[0]
    chunks = range(tm // CHUNK)
    heads = range(GLA_HEADS)

    @pl.when(pl.program_id(1) == 0)
    def _():
        st_ref[...] = jnp.zeros_like(st_ref)

    causal = _lower_tri(CHUNK)
    lane_head = lax.broadcasted_iota(jnp.int32, (1, GLA_KW), 1) // GLA_DK
    ng = ng_ref[...]
    rows = [slice(c * CHUNK, (c + 1) * CHUNK) for c in chunks]
    pair_rows = [slice((c // 2) * 2 * CHUNK, (c // 2 + 1) * 2 * CHUNK) for c in chunks]
    vcol = [slice(2 * GLA_KW + hd * GLA_DV, 2 * GLA_KW + (hd + 1) * GLA_DV) for hd in heads]

    b_all = _cumsum_rows_scan(la_ref[...], period=CHUNK)
    bs = [b_all[rows[c], :] for c in chunks]

    q_e, k_e, k_d, dec = [], [], [], []
    for c in chunks:
        b_last = bs[c][CHUNK - 1:CHUNK, :]
        q = gin_ref[rows[c], 0:GLA_KW].astype(F32) * (GLA_DK ** -0.5)
        k = gin_ref[rows[c], GLA_KW:2 * GLA_KW].astype(F32)
        q_e.append((q * jnp.exp(bs[c])).astype(BF16))
        k_e.append((k * jnp.exp(-bs[c])).astype(BF16))
        k_d.append(k * jnp.exp(b_last - bs[c]))
        dec.append(jnp.transpose(jnp.broadcast_to(jnp.exp(b_last), (LANES, GLA_KW))))

    q_h = [[jnp.where(lane_head == hd, q_e[c], jnp.zeros_like(q_e[c])) for hd in heads] for c in chunks]
    s = [[jnp.where(causal, _dot_nt(q_h[c][hd], k_e[c]), 0.0).astype(BF16) for hd in heads] for c in chunks]
    o_intra = [[_dot(s[c][hd], gin_ref[rows[c], vcol[hd]]) for hd in heads] for c in chunks]

    token_lane = lax.broadcasted_iota(jnp.int32, (1, 2 * CHUNK), 1) // CHUNK
    k_dt = []
    for c in chunks[::2]:
        both = jnp.transpose(jnp.concatenate([k_d[c], k_d[c + 1]], axis=0))
        k_dt += [jnp.where(token_lane == half, both, 0.0).astype(BF16) for half in range(2)]
    kv = [jnp.concatenate([_dot(k_dt[c][hd * GLA_DK:(hd + 1) * GLA_DK, :], gin_ref[pair_rows[c], vcol[hd]])
                           for hd in heads], axis=0) for c in chunks]

    state = st_ref[...]
    states = []
    for c in chunks:
        states.append(state.astype(BF16))
        state = dec[c] * state + kv[c]
    st_ref[...] = state

    for c in chunks:
        r = gin_ref[rows[c], 2 * GLA_KW + GLA_WIDTH:].astype(F32)
        outs = [_rms(o_intra[c][hd] + _dot(q_h[c][hd], states[c]), ng[:, hd * GLA_DV:(hd + 1) * GLA_DV])
                for hd in heads]
        o_ref[rows[c], :] = (jnp.concatenate(outs, axis=1) * (r * jax.nn.sigmoid(r))).astype(o_ref.dtype)


def _gla_call(gin, la, ng, batch, seq):
    tm = GLA_TILE
    n = seq // tm
    rows = lambda b, i: (b * n + i, 0)
    return pl.pallas_call(
        _gla_kernel,
        grid=(batch, n),
        in_specs=[
            pl.BlockSpec((tm, SEC_GLA), rows),
            pl.BlockSpec((tm, GLA_KW), rows),
            pl.BlockSpec((1, GLA_WIDTH), lambda b, i: (0, 0)),
        ],
        out_specs=pl.BlockSpec((tm, GLA_WIDTH), rows),
        out_shape=jax.ShapeDtypeStruct((batch * seq, GLA_WIDTH), BF16),
        scratch_shapes=[pltpu.VMEM((GLA_KW, GLA_DV), F32)],
        compiler_params=_GRID_PARAMS,
        name="gla",
    )(gin, la, ng)


def _fox_kernel(qt_ref, k_ref, vt_ref, ct_ref, o_ref, s_ref, p_ref, acc_ref):
    grp = pl.program_id(1)
    seq = k_ref.shape[0]
    blk = vt_ref.shape[3]
    nq = seq // blk
    heads = range(FOX_GROUP)
    lanes = [slice(j * LANES, (j + 1) * LANES) for j in heads]

    def scores(qi, ki):
        k0 = pl.multiple_of(ki * blk, blk)
        col_max = []
        for j in heads:
            s = _dot(k_ref[pl.ds(k0, blk), lanes[j]], qt_ref[0, qi, lanes[j], :])
            s_ref[j] = s
            col_max.append(jnp.max(s, axis=0, keepdims=True))
        return col_max

    def probs(cqs, ms, col_max):
        new_ms, alphas = [], []
        for j in heads:
            m_new = jnp.maximum(ms[j], col_max[j] + cqs[j])
            p_ref[j] = jnp.exp2(s_ref[j] - (m_new - cqs[j])).astype(BF16)
            alphas.append(jnp.exp2(ms[j] - m_new))
            new_ms.append(m_new)
        return new_ms, alphas

    def accumulate(ki, alphas):
        for j in heads:
            acc_ref[j] = alphas[j] * acc_ref[j] + _dot(vt_ref[0, ki, j * VT_ROWS:(j + 1) * VT_ROWS, :], p_ref[j])

    hb = blk // 2
    lo, hi = slice(0, hb), slice(hb, blk)

    def diagonal_probs(cqs, ms):
        tri = (lax.broadcasted_iota(jnp.int32, (hb, hb), 0)
               <= lax.broadcasted_iota(jnp.int32, (hb, hb), 1))
        alphas = []
        for j in heads:
            s00 = jnp.where(tri, s_ref[j, lo, lo], NEG)
            s01 = s_ref[j, lo, hi]
            s11 = jnp.where(tri, s_ref[j, hi, hi], NEG)
            cm = jnp.concatenate(
                [jnp.max(s00, axis=0, keepdims=True),
                 jnp.maximum(jnp.max(s01, axis=0, keepdims=True), jnp.max(s11, axis=0, keepdims=True))], axis=1)
            m_new = jnp.maximum(ms[j], cm + cqs[j])
            ref = m_new - cqs[j]
            p_ref[j, lo, lo] = jnp.exp2(s00 - ref[:, lo]).astype(BF16)
            p_ref[j, lo, hi] = jnp.exp2(s01 - ref[:, hi]).astype(BF16)
            p_ref[j, hi, hi] = jnp.exp2(s11 - ref[:, hi]).astype(BF16)
            alphas.append(jnp.exp2(ms[j] - m_new))
        return alphas

    def diagonal_accumulate(ki, alphas):
        for j in heads:
            vt = vt_ref[0, ki, j * VT_ROWS:(j + 1) * VT_ROWS, :]
            acc_ref[j, :, lo] = alphas[j][:, lo] * acc_ref[j, :, lo] + _dot(vt[:, lo], p_ref[j, lo, lo])
            acc_ref[j, :, hi] = alphas[j][:, hi] * acc_ref[j, :, hi] + _dot(vt, p_ref[j, :, hi])

    def q_body(qi, col_max):
        cqs = [ct_ref[0, qi, pl.ds(FOX_GROUP * grp + j, 1), :] * LOG2E for j in heads]
        acc_ref[...] = jnp.zeros_like(acc_ref)

        def kv_step(ki, state):
            ms, col_max = state
            ms, alphas = probs(cqs, ms, col_max)
            col_max = scores(qi, ki + 1)
            accumulate(ki, alphas)
            return ms, col_max

        ms, _ = lax.fori_loop(0, qi, kv_step, ([jnp.full((1, blk), NEG, F32) for _ in heads], col_max))
        alphas = diagonal_probs(cqs, ms)
        col_max = scores(jnp.minimum(qi + 1, nq - 1), 0)
        diagonal_accumulate(qi, alphas)
        o_t = jnp.concatenate([acc_ref[j, 0:FOX_HD, :] / acc_ref[j, FOX_HD:FOX_HD + 1, :] for j in heads], axis=0)
        o_ref[pl.ds(pl.multiple_of(qi * blk, blk), blk), :] = jnp.transpose(o_t).astype(o_ref.dtype)
        return col_max

    lax.fori_loop(0, nq, q_body, scores(0, 0))


def _fox_call(fq, fk, vt, ct4, batch, seq):
    blk = ROW_TILE
    groups = FOX_HEADS // FOX_GROUP
    return pl.pallas_call(
        _fox_kernel,
        grid=(batch, groups),
        in_specs=[
            pl.BlockSpec((1, seq // blk, FOX_GROUP * LANES, blk), lambda b, p: (b, 0, p, 0)),
            pl.BlockSpec((seq, FOX_GROUP * LANES), lambda b, p: (b, p)),
            pl.BlockSpec((1, seq // blk, FOX_GROUP * VT_ROWS, blk), lambda b, p: (b, 0, p, 0)),
            pl.BlockSpec((1, seq // blk, FOX_HEADS, blk), lambda b, p: (b, 0, 0, 0)),
        ],
        out_specs=pl.BlockSpec((seq, FOX_GROUP * FOX_HD), lambda b, p: (b, p)),
        out_shape=jax.ShapeDtypeStruct((batch * seq, FOX_WIDTH), BF16),
        scratch_shapes=[
            pltpu.VMEM((FOX_GROUP, blk, blk), F32),
            pltpu.VMEM((FOX_GROUP, blk, blk), BF16),
            pltpu.VMEM((FOX_GROUP, VT_ROWS, blk), F32),
        ],
        compiler_params=_GRID_PARAMS,
        name="fox",
    )(fq, fk, vt, ct4)


def _gelu_tanh(a):
    return 0.5 * a * (1.0 + jnp.tanh(0.7978845608028654 * (a + 0.044715 * (a * a * a))))


def _ffn_kernel(x_ref, og_ref, of_ref, gate_ref, wg_ref, wf_ref, wo_ref, n2_ref,
                wu_ref, cw_ref, cb_ref, wd_ref, nf_ref, out_ref,
                carry_ref, stage_ref, act_ref):
    tm = x_ref.shape[0]

    @pl.when(pl.program_id(1) == 0)
    def _():
        carry_ref[...] = jnp.zeros_like(carry_ref)

    y = (gate_ref[:, 0:D_MODEL].astype(F32) * _dot(og_ref[...], wg_ref[...])
         + gate_ref[:, D_MODEL:].astype(F32) * _dot(of_ref[...], wf_ref[...]))
    x1 = x_ref[...] + _dot(y.astype(BF16), wo_ref[...])
    out_ref[...] = x1
    h2 = _rms(x1, n2_ref[...]).astype(BF16)

    for j in range(N_FF_CHUNKS):
        cols = slice(j * FF_CHUNK, (j + 1) * FF_CHUNK)
        ua = _dot(h2, wu_ref[:, cols])
        uv = _dot(h2, wu_ref[:, D_FF + j * FF_CHUNK:D_FF + (j + 1) * FF_CHUNK])
        slot = j % stage_ref.shape[0]
        stage_ref[slot, 0:SUBLANES, :] = carry_ref[:, cols]
        stage_ref[slot, SUBLANES:, :] = ua
        carry_ref[:, cols] = ua[tm - SUBLANES:, :]
        a = (cb_ref[:, cols] + cw_ref[0:1, cols] * stage_ref[slot, SUBLANES - 2:SUBLANES - 2 + tm, :]
             + cw_ref[1:2, cols] * stage_ref[slot, SUBLANES - 1:SUBLANES - 1 + tm, :] + cw_ref[2:3, cols] * ua)
        act_ref[:, cols] = (_gelu_tanh(a) * uv).astype(BF16)

    out_ref[...] = _rms(out_ref[...] + _dot(act_ref[...], wd_ref[...]), nf_ref[...])


def _ffn_call(x2, og, of, gate, wg, wf, wo, n2, wu, cw, cb, wd, nf, batch, seq):
    tm = ROW_TILE
    n = seq // tm
    rows = lambda b, i: (b * n + i, 0)
    return pl.pallas_call(
        _ffn_kernel,
        grid=(batch, n),
        in_specs=[
            pl.BlockSpec((tm, D_MODEL), rows),
            pl.BlockSpec((tm, GLA_WIDTH), rows),
            pl.BlockSpec((tm, FOX_WIDTH), rows),
            pl.BlockSpec((tm, 2 * D_MODEL), rows),
            _resident((GLA_WIDTH, D_MODEL)),
            _resident((FOX_WIDTH, D_MODEL)),
            _resident((D_MODEL, D_MODEL)),
            _resident((1, D_MODEL)),
            _resident((D_MODEL, 2 * D_FF)),
            _resident((SUBLANES, D_FF)),
            _resident((1, D_FF)),
            _resident((D_FF, D_MODEL)),
            _resident((1, D_MODEL)),
        ],
        out_specs=pl.BlockSpec((tm, D_MODEL), rows),
        out_shape=jax.ShapeDtypeStruct((batch * seq, D_MODEL), F32),
        scratch_shapes=[
            pltpu.VMEM((SUBLANES, D_FF), F32),
            pltpu.VMEM((2, tm + SUBLANES, FF_CHUNK), F32),
            pltpu.VMEM((tm, D_FF), BF16),
        ],
        compiler_params=_GRID_PARAMS,
        name="ffn",
    )(x2, og, of, gate, wg, wf, wo, n2, wu, cw, cb, wd, nf)


def _layer(x2, batch, seq, norm_mix_g, w_in, gla_alpha_w2, gla_alpha_b, gla_out_norm_g,
           fox_forget_b, fox_q_norm_g, fox_k_norm_g, gate_b, w_gla_branch, w_fox_branch,
           w_out, norm_ffn_g, w_up, conv_w, conv_b, w_down, final_g):
    offs = [0]
    for s in IN_SPLITS:
        offs.append(offs[-1] + s)
    sec = lambda a, b: w_in[:, offs[a]:offs[b]]
    small_pad = LANES - LR_COPIES * GLA_LOWRANK - N_SPLIT * FOX_HEADS
    w_small = jnp.concatenate([sec(4, 5)] * LR_COPIES + [sec(8, 9)] * N_SPLIT
                              + [jnp.zeros((D_MODEL, small_pad), w_in.dtype)], axis=1).astype(BF16)
    w_gla, w_fox, w_gate = sec(0, 4).astype(BF16), sec(5, 8).astype(BF16), sec(9, 11).astype(BF16)
    w2_hi, w2_mid, w2_lo = _split3(gla_alpha_w2)
    a2 = jnp.concatenate([w2_hi, w2_mid, w2_hi, w2_lo, w2_mid, w2_hi,
                          jnp.zeros((LANES - OFF_FF_IN_SMALL, GLA_KW), BF16)], axis=0)
    fb = jnp.concatenate([jnp.zeros((OFF_FF_IN_SMALL,), F32)] + [fox_forget_b] * N_SPLIT
                         + [jnp.zeros((small_pad,), F32)]).reshape(1, LANES)
    qg = (jnp.tile(fox_q_norm_g, FOX_HEADS) * (FOX_HD ** -0.5 * LOG2E)).reshape(1, FOX_WIDTH)
    kg = jnp.tile(fox_k_norm_g, FOX_HEADS).reshape(1, FOX_WIDTH)
    src = jnp.arange(LANES)[:, None] - OFF_FF_IN_SMALL
    col = jnp.arange(FOX_SPREAD)[None, :]
    place = ((src >= 0) & (src < N_SPLIT * FOX_HEADS)
             & (col == (src % FOX_HEADS) * LANES + FOX_HD + src // FOX_HEADS)).astype(BF16)

    gla_in, la, fq, fk, vt, ct, gates = _proj_call(
        x2, norm_mix_g.reshape(1, D_MODEL), w_gla, w_small, w_fox, w_gate, a2, gla_alpha_b.reshape(1, GLA_KW),
        fb, qg, kg, gate_b.reshape(1, 2 * D_MODEL), place, batch, seq)

    o_gla = _gla_call(gla_in, la, gla_out_norm_g.reshape(1, GLA_WIDTH), batch, seq)

    o_fox = _fox_call(fq, fk, vt, ct, batch, seq)

    cw = jnp.pad(conv_w, ((0, SUBLANES - CONV_W), (0, 0)))
    return _ffn_call(
        x2, o_gla, o_fox, gates, w_gla_branch.astype(BF16), w_fox_branch.astype(BF16), w_out.astype(BF16),
        norm_ffn_g.reshape(1, D_MODEL), w_up.astype(BF16), cw, conv_b.reshape(1, D_FF), w_down.astype(BF16),
        final_g.reshape(1, D_MODEL), batch, seq)


def kernel(x, norm_mix_g, w_in, gla_alpha_w2, gla_alpha_b, gla_out_norm_g, fox_forget_b, fox_q_norm_g,
           fox_k_norm_g, gate_b, w_gla_branch, w_fox_branch, w_out, norm_ffn_g, w_up, conv_w, conv_b,
           w_down, norm_final_g):
    batch, seq, _ = x.shape
    assert norm_mix_g.shape[0] == 1, "the final RMSNorm is fused into the single layer's last kernel"
    assert seq % ROW_TILE == 0 and seq % GLA_TILE == 0
    out = _layer(x.reshape(batch * seq, D_MODEL), batch, seq, norm_mix_g[0], w_in[0], gla_alpha_w2[0],
                 gla_alpha_b[0], gla_out_norm_g[0], fox_forget_b[0], fox_q_norm_g[0], fox_k_norm_g[0],
                 gate_b[0], w_gla_branch[0], w_fox_branch[0], w_out[0], norm_ffn_g[0], w_up[0],
                 conv_w[0], conv_b[0], w_down[0], norm_final_g)
    return out.reshape(batch, seq, D_MODEL)
```

```python
import jax
import jax.numpy as jnp
from jax import lax
from jax.experimental import pallas as pl
from jax.experimental.pallas import tpu as pltpu

D_MODEL = 1024
CHUNK = 64
GLA_HEADS = 4
GLA_DK = 64
GLA_DV = 128
GLA_LOWRANK = 16
GLA_TAU = 16.0
FOX_HEADS = 8
FOX_HD = 64
D_FF = 2816
CONV_W = 3
EPS = 1e-6

GLA_KW = GLA_HEADS * GLA_DK
GLA_WIDTH = GLA_HEADS * GLA_DV
FOX_WIDTH = FOX_HEADS * FOX_HD
IN_SPLITS = (GLA_KW, GLA_KW, GLA_WIDTH, GLA_WIDTH, GLA_LOWRANK,
             FOX_WIDTH, FOX_WIDTH, FOX_WIDTH, FOX_HEADS, D_MODEL, D_MODEL)

LANES = 128
SUBLANES = 8
BF16_ROWS = 16
MXU_N = 256
VMEM_LIMIT = 56 * 1024 * 1024

N_SPLIT = 3
LOG2E = 1.4426950408889634

SEC_GLA = 2 * GLA_KW + 2 * GLA_WIDTH
LR_COPIES = 6
OFF_FF_IN_SMALL = LR_COPIES * GLA_LOWRANK
FOX_SPREAD = FOX_HEADS * LANES

ROW_TILE = 512
GLA_TILE = 2048
FF_CHUNK = MXU_N
N_FF_CHUNKS = D_FF // FF_CHUNK
FOX_GROUP = 4
VT_ROWS = FOX_HD + BF16_ROWS

BF16 = jnp.bfloat16
F32 = jnp.float32
NEG = -1e30


def _dot(a, b):
    return jnp.dot(a, b, preferred_element_type=F32)


def _dot_nt(a, b):
    return lax.dot_general(a, b, (((1,), (1,)), ((), ())), preferred_element_type=F32)


def _rms(x, g):
    return x * lax.rsqrt(jnp.mean(x * x, axis=-1, keepdims=True) + EPS) * g


def _log_sigmoid(z):
    return jnp.minimum(z, 0.0) - jnp.log1p(jnp.exp(-jnp.abs(z)))


def _split3(a):
    hi = a.astype(BF16)
    r1 = a - hi.astype(F32)
    mid = r1.astype(BF16)
    lo = (r1 - mid.astype(F32)).astype(BF16)
    return hi, mid, lo


def _cumsum_rows_scan(x, period=None):
    n, w = x.shape
    period = period or n
    row = lax.broadcasted_iota(jnp.int32, (n, w), 0) & (period - 1)
    d = 1
    while d < period:
        if d < SUBLANES:
            shifted = jnp.where(row >= d, pltpu.roll(x, d, axis=0), 0.0)
        else:
            shifted = jnp.concatenate([jnp.zeros((d, w), F32), x[:n - d]], axis=0)
            if period < n:
                shifted = jnp.where(row >= d, shifted, 0.0)
        x = x + shifted
        d *= 2
    return x


def _lower_tri(n):
    r = lax.broadcasted_iota(jnp.int32, (n, n), 0)
    c = lax.broadcasted_iota(jnp.int32, (n, n), 1)
    return r >= c


def _resident(shape):
    return pl.BlockSpec(shape, lambda b, i: (0,) * len(shape), pipeline_mode=pl.Buffered(1))


_GRID_PARAMS = pltpu.CompilerParams(
    dimension_semantics=("arbitrary", "arbitrary"), vmem_limit_bytes=VMEM_LIMIT)


def _proj_kernel(x_ref, g_ref, w_gla_ref, w_small_ref, w_fox_ref, w_gate_ref,
                 a2_ref, ab_ref, fb_ref, qg_ref, kg_ref, gb_ref, place_ref,
                 gla_ref, la_ref, fq_ref, fk_ref, vt_ref, ct_ref, gate_ref, carry_ref):
    tm = x_ref.shape[0]

    @pl.when(pl.program_id(1) == 0)
    def _():
        carry_ref[...] = jnp.zeros_like(carry_ref)

    h = _rms(x_ref[...], g_ref[...]).astype(BF16)

    small = _dot(h, w_small_ref[...])
    gate_ref[...] = jax.nn.sigmoid(_dot(h, w_gate_ref[...]) + gb_ref[...]).astype(BF16)
    lane = lax.broadcasted_iota(jnp.int32, (1, LANES), 1)

    s_hi, s_mid, s_lo = _split3(small)
    grp = lane // GLA_LOWRANK
    glr_terms = jnp.where((grp == 2) | (grp == 4), s_mid, jnp.where(grp == 5, s_lo, s_hi))
    z = _dot(glr_terms, a2_ref[...]) + ab_ref[...]
    la_ref[...] = _log_sigmoid(z) * (1.0 / GLA_TAU)

    lf = _log_sigmoid(small + fb_ref[...])
    c = _cumsum_rows_scan(lf) + carry_ref[...]
    carry_ref[...] = c[tm - 1:tm, :]
    ct_ref[0, 0] = jnp.transpose(c)[OFF_FF_IN_SMALL:OFF_FF_IN_SMALL + FOX_HEADS, :]
    c_hi, c_mid, c_lo = _split3(c * (-LOG2E))
    c_terms = jnp.where(lane < OFF_FF_IN_SMALL + FOX_HEADS, c_hi,
                        jnp.where(lane < OFF_FF_IN_SMALL + 2 * FOX_HEADS, c_mid, c_lo))
    c_place = _dot(c_terms, place_ref[...])

    ones = jnp.where((lane >= FOX_HD) & (lane < FOX_HD + N_SPLIT), 1.0, 0.0)
    low = lane < FOX_HD

    def pair_norm(t, g):
        t2 = t * t
        ms_a = jnp.sum(jnp.where(low, t2, 0.0), axis=-1, keepdims=True) * (1.0 / FOX_HD)
        ms_b = jnp.sum(jnp.where(low, 0.0, t2), axis=-1, keepdims=True) * (1.0 / FOX_HD)
        return t * jnp.where(low, lax.rsqrt(ms_a + EPS), lax.rsqrt(ms_b + EPS)) * g

    fqk = _dot(h, w_fox_ref[:, 0:2 * FOX_WIDTH])
    for pr in range(FOX_HEADS // 2):
        qn = pair_norm(fqk[:, pr * LANES:(pr + 1) * LANES], qg_ref[:, pr * LANES:(pr + 1) * LANES])
        kn = pair_norm(fqk[:, FOX_WIDTH + pr * LANES:FOX_WIDTH + (pr + 1) * LANES],
                       kg_ref[:, pr * LANES:(pr + 1) * LANES])
        for half, (qh, kh) in enumerate(((qn, kn), (pltpu.roll(qn, FOX_HD, axis=1), pltpu.roll(kn, FOX_HD, axis=1)))):
            sl = slice((2 * pr + half) * LANES, (2 * pr + half + 1) * LANES)
            fq_ref[0, 0, sl, :] = jnp.transpose(jnp.where(low, qh, ones)).astype(BF16)
            fk_ref[:, sl] = jnp.where(low, kh, c_place[:, sl]).astype(BF16)

    fvt = jnp.transpose(_dot(h, w_fox_ref[:, 2 * FOX_WIDTH:])).astype(BF16)
    for hd in range(FOX_HEADS):
        vt_ref[0, 0, hd * VT_ROWS:hd * VT_ROWS + FOX_HD, :] = fvt[hd * FOX_HD:(hd + 1) * FOX_HD, :]
        vt_ref[0, 0, hd * VT_ROWS + FOX_HD:(hd + 1) * VT_ROWS, :] = jnp.ones((BF16_ROWS, tm), BF16)

    gla_ref[...] = _dot(h, w_gla_ref[...]).astype(BF16)


def _proj_call(x2, g, w_gla, w_small, w_fox, w_gate, a2, ab, fb, qg, kg, gb, place, batch, seq):
    tm = ROW_TILE
    n = seq // tm
    rows = lambda b, i: (b * n + i, 0)
    tokens = batch * seq
    return pl.pallas_call(
        _proj_kernel,
        grid=(batch, n),
        in_specs=[
            pl.BlockSpec((tm, D_MODEL), rows),
            _resident((1, D_MODEL)),
            _resident((D_MODEL, SEC_GLA)),
            _resident((D_MODEL, LANES)),
            _resident((D_MODEL, 3 * FOX_WIDTH)),
            _resident((D_MODEL, 2 * D_MODEL)),
            _resident((LANES, GLA_KW)),
            _resident((1, GLA_KW)),
            _resident((1, LANES)),
            _resident((1, FOX_WIDTH)),
            _resident((1, FOX_WIDTH)),
            _resident((1, 2 * D_MODEL)),
            _resident((LANES, FOX_SPREAD)),
        ],
        out_specs=[
            pl.BlockSpec((tm, SEC_GLA), rows),
            pl.BlockSpec((tm, GLA_KW), rows),
            pl.BlockSpec((1, 1, FOX_SPREAD, tm), lambda b, i: (b, i, 0, 0)),
            pl.BlockSpec((tm, FOX_SPREAD), rows),
            pl.BlockSpec((1, 1, FOX_HEADS * VT_ROWS, tm), lambda b, i: (b, i, 0, 0)),
            pl.BlockSpec((1, 1, FOX_HEADS, tm), lambda b, i: (b, i, 0, 0)),
            pl.BlockSpec((tm, 2 * D_MODEL), rows),
        ],
        out_shape=[
            jax.ShapeDtypeStruct((tokens, SEC_GLA), BF16),
            jax.ShapeDtypeStruct((tokens, GLA_KW), F32),
            jax.ShapeDtypeStruct((batch, n, FOX_SPREAD, tm), BF16),
            jax.ShapeDtypeStruct((tokens, FOX_SPREAD), BF16),
            jax.ShapeDtypeStruct((batch, n, FOX_HEADS * VT_ROWS, tm), BF16),
            jax.ShapeDtypeStruct((batch, n, FOX_HEADS, tm), F32),
            jax.ShapeDtypeStruct((tokens, 2 * D_MODEL), BF16),
        ],
        scratch_shapes=[pltpu.VMEM((1, LANES), F32)],
        compiler_params=_GRID_PARAMS,
        name="proj",
    )(x2, g, w_gla, w_small, w_fox, w_gate, a2, ab, fb, qg, kg, gb, place)


def _gla_kernel(gin_ref, la_ref, ng_ref, o_ref, st_ref):
    tm = gin_ref.shape[0]
    chunks = range(tm // CHUNK)
    heads = range(GLA_HEADS)

    @pl.when(pl.program_id(1) == 0)
    def _():
        st_ref[...] = jnp.zeros_like(st_ref)

    causal = _lower_tri(CHUNK)
    lane_head = lax.broadcasted_iota(jnp.int32, (1, GLA_KW), 1) // GLA_DK
    ng = ng_ref[...]
    rows = [slice(c * CHUNK, (c + 1) * CHUNK) for c in chunks]
    pair_rows = [slice((c // 2) * 2 * CHUNK, (c // 2 + 1) * 2 * CHUNK) for c in chunks]
    vcol = [slice(2 * GLA_KW + hd * GLA_DV, 2 * GLA_KW + (hd + 1) * GLA_DV) for hd in heads]

    b_all = _cumsum_rows_scan(la_ref[...], period=CHUNK)
    bs = [b_all[rows[c], :] for c in chunks]

    q_e, k_e, k_d, dec = [], [], [], []
    for c in chunks:
        b_last = bs[c][CHUNK - 1:CHUNK, :]
        q = gin_ref[rows[c], 0:GLA_KW].astype(F32) * (GLA_DK ** -0.5)
        k = gin_ref[rows[c], GLA_KW:2 * GLA_KW].astype(F32)
        q_e.append((q * jnp.exp(bs[c])).astype(BF16))
        k_e.append((k * jnp.exp(-bs[c])).astype(BF16))
        k_d.append(k * jnp.exp(b_last - bs[c]))
        dec.append(jnp.transpose(jnp.broadcast_to(jnp.exp(b_last), (LANES, GLA_KW))))

    q_h = [[jnp.where(lane_head == hd, q_e[c], jnp.zeros_like(q_e[c])) for hd in heads] for c in chunks]
    s = [[jnp.where(causal, _dot_nt(q_h[c][hd], k_e[c]), 0.0).astype(BF16) for hd in heads] for c in chunks]
    o_intra = [[_dot(s[c][hd], gin_ref[rows[c], vcol[hd]]) for hd in heads] for c in chunks]

    token_lane = lax.broadcasted_iota(jnp.int32, (1, 2 * CHUNK), 1) // CHUNK
    k_dt = []
    for c in chunks[::2]:
        both = jnp.transpose(jnp.concatenate([k_d[c], k_d[c + 1]], axis=0))
        k_dt += [jnp.where(token_lane == half, both, 0.0).astype(BF16) for half in range(2)]
    kv = [jnp.concatenate([_dot(k_dt[c][hd * GLA_DK:(hd + 1) * GLA_DK, :], gin_ref[pair_rows[c], vcol[hd]])
                           for hd in heads], axis=0) for c in chunks]

    state = st_ref[...]
    states = []
    for c in chunks:
        states.append(state.astype(BF16))
        state = dec[c] * state + kv[c]
    st_ref[...] = state

    for c in chunks:
        r = gin_ref[rows[c], 2 * GLA_KW + GLA_WIDTH:].astype(F32)
        outs = [_rms(o_intra[c][hd] + _dot(q_h[c][hd], states[c]), ng[:, hd * GLA_DV:(hd + 1) * GLA_DV])
                for hd in heads]
        o_ref[rows[c], :] = (jnp.concatenate(outs, axis=1) * (r * jax.nn.sigmoid(r))).astype(o_ref.dtype)


def _gla_call(gin, la, ng, batch, seq):
    tm = GLA_TILE
    n = seq // tm
    rows = lambda b, i: (b * n + i, 0)
    return pl.pallas_call(
        _gla_kernel,
        grid=(batch, n),
        in_specs=[
            pl.BlockSpec((tm, SEC_GLA), rows),
            pl.BlockSpec((tm, GLA_KW), rows),
            pl.BlockSpec((1, GLA_WIDTH), lambda b, i: (0, 0)),
        ],
        out_specs=pl.BlockSpec((tm, GLA_WIDTH), rows),
        out_shape=jax.ShapeDtypeStruct((batch * seq, GLA_WIDTH), BF16),
        scratch_shapes=[pltpu.VMEM((GLA_KW, GLA_DV), F32)],
        compiler_params=_GRID_PARAMS,
        name="gla",
    )(gin, la, ng)


def _fox_kernel(qt_ref, k_ref, vt_ref, ct_ref, o_ref, s_ref, p_ref, acc_ref):
    grp = pl.program_id(1)
    seq = k_ref.shape[0]
    blk = vt_ref.shape[3]
    nq = seq // blk
    heads = range(FOX_GROUP)
    lanes = [slice(j * LANES, (j + 1) * LANES) for j in heads]

    def scores(qi, ki):
        k0 = pl.multiple_of(ki * blk, blk)
        col_max = []
        for j in heads:
            s = _dot(k_ref[pl.ds(k0, blk), lanes[j]], qt_ref[0, qi, lanes[j], :])
            s_ref[j] = s
            col_max.append(jnp.max(s, axis=0, keepdims=True))
        return col_max

    def probs(cqs, ms, col_max):
        new_ms, alphas = [], []
        for j in heads:
            m_new = jnp.maximum(ms[j], col_max[j] + cqs[j])
            p_ref[j] = jnp.exp2(s_ref[j] - (m_new - cqs[j])).astype(BF16)
            alphas.append(jnp.exp2(ms[j] - m_new))
            new_ms.append(m_new)
        return new_ms, alphas

    def accumulate(ki, alphas):
        for j in heads:
            acc_ref[j] = alphas[j] * acc_ref[j] + _dot(vt_ref[0, ki, j * VT_ROWS:(j + 1) * VT_ROWS, :], p_ref[j])

    hb = blk // 2
    lo, hi = slice(0, hb), slice(hb, blk)

    def diagonal_probs(cqs, ms):
        tri = (lax.broadcasted_iota(jnp.int32, (hb, hb), 0)
               <= lax.broadcasted_iota(jnp.int32, (hb, hb), 1))
        alphas = []
        for j in heads:
            s00 = jnp.where(tri, s_ref[j, lo, lo], NEG)
            s01 = s_ref[j, lo, hi]
            s11 = jnp.where(tri, s_ref[j, hi, hi], NEG)
            cm = jnp.concatenate(
                [jnp.max(s00, axis=0, keepdims=True),
                 jnp.maximum(jnp.max(s01, axis=0, keepdims=True), jnp.max(s11, axis=0, keepdims=True))], axis=1)
            m_new = jnp.maximum(ms[j], cm + cqs[j])
            ref = m_new - cqs[j]
            p_ref[j, lo, lo] = jnp.exp2(s00 - ref[:, lo]).astype(BF16)
            p_ref[j, lo, hi] = jnp.exp2(s01 - ref[:, hi]).astype(BF16)
            p_ref[j, hi, hi] = jnp.exp2(s11 - ref[:, hi]).astype(BF16)
            alphas.append(jnp.exp2(ms[j] - m_new))
        return alphas

    def diagonal_accumulate(ki, alphas):
        for j in heads:
            vt = vt_ref[0, ki, j * VT_ROWS:(j + 1) * VT_ROWS, :]
            acc_ref[j, :, lo] = alphas[j][:, lo] * acc_ref[j, :, lo] + _dot(vt[:, lo], p_ref[j, lo, lo])
            acc_ref[j, :, hi] = alphas[j][:, hi] * acc_ref[j, :, hi] + _dot(vt, p_ref[j, :, hi])

    def q_body(qi, col_max):
        cqs = [ct_ref[0, qi, pl.ds(FOX_GROUP * grp + j, 1), :] * LOG2E for j in heads]
        acc_ref[...] = jnp.zeros_like(acc_ref)

        def kv_step(ki, state):
            ms, col_max = state
            ms, alphas = probs(cqs, ms, col_max)
            col_max = scores(qi, ki + 1)
            accumulate(ki, alphas)
            return ms, col_max

        ms, _ = lax.fori_loop(0, qi, kv_step, ([jnp.full((1, blk), NEG, F32) for _ in heads], col_max))
        alphas = diagonal_probs(cqs, ms)
        col_max = scores(jnp.minimum(qi + 1, nq - 1), 0)
        diagonal_accumulate(qi, alphas)
        o_t = jnp.concatenate([acc_ref[j, 0:FOX_HD, :] / acc_ref[j, FOX_HD:FOX_HD + 1, :] for j in heads], axis=0)
        o_ref[pl.ds(pl.multiple_of(qi * blk, blk), blk), :] = jnp.transpose(o_t).astype(o_ref.dtype)
        return col_max

    lax.fori_loop(0, nq, q_body, scores(0, 0))


def _fox_call(fq, fk, vt, ct4, batch, seq):
    blk = ROW_TILE
    groups = FOX_HEADS // FOX_GROUP
    return pl.pallas_call(
        _fox_kernel,
        grid=(batch, groups),
        in_specs=[
            pl.BlockSpec((1, seq // blk, FOX_GROUP * LANES, blk), lambda b, p: (b, 0, p, 0)),
            pl.BlockSpec((seq, FOX_GROUP * LANES), lambda b, p: (b, p)),
            pl.BlockSpec((1, seq // blk, FOX_GROUP * VT_ROWS, blk), lambda b, p: (b, 0, p, 0)),
            pl.BlockSpec((1, seq // blk, FOX_HEADS, blk), lambda b, p: (b, 0, 0, 0)),
        ],
        out_specs=pl.BlockSpec((seq, FOX_GROUP * FOX_HD), lambda b, p: (b, p)),
        out_shape=jax.ShapeDtypeStruct((batch * seq, FOX_WIDTH), BF16),
        scratch_shapes=[
            pltpu.VMEM((FOX_GROUP, blk, blk), F32),
            pltpu.VMEM((FOX_GROUP, blk, blk), BF16),
            pltpu.VMEM((FOX_GROUP, VT_ROWS, blk), F32),
        ],
        compiler_params=_GRID_PARAMS,
        name="fox",
    )(fq, fk, vt, ct4)


def _gelu_tanh(a):
    return 0.5 * a * (1.0 + jnp.tanh(0.7978845608028654 * (a + 0.044715 * (a * a * a))))


def _ffn_kernel(x_ref, og_ref, of_ref, gate_ref, wg_ref, wf_ref, wo_ref, n2_ref,
                wu_ref, cw_ref, cb_ref, wd_ref, nf_ref, out_ref,
                carry_ref, stage_ref, act_ref):
    tm = x_ref.shape[0]

    @pl.when(pl.program_id(1) == 0)
    def _():
        carry_ref[...] = jnp.zeros_like(carry_ref)

    y = (gate_ref[:, 0:D_MODEL].astype(F32) * _dot(og_ref[...], wg_ref[...])
         + gate_ref[:, D_MODEL:].astype(F32) * _dot(of_ref[...], wf_ref[...]))
    x1 = x_ref[...] + _dot(y.astype(BF16), wo_ref[...])
    out_ref[...] = x1
    h2 = _rms(x1, n2_ref[...]).astype(BF16)

    for j in range(N_FF_CHUNKS):
        cols = slice(j * FF_CHUNK, (j + 1) * FF_CHUNK)
        ua = _dot(h2, wu_ref[:, cols])
        uv = _dot(h2, wu_ref[:, D_FF + j * FF_CHUNK:D_FF + (j + 1) * FF_CHUNK])
        slot = j % stage_ref.shape[0]
        stage_ref[slot, 0:SUBLANES, :] = carry_ref[:, cols]
        stage_ref[slot, SUBLANES:, :] = ua
        carry_ref[:, cols] = ua[tm - SUBLANES:, :]
        a = (cb_ref[:, cols] + cw_ref[0:1, cols] * stage_ref[slot, SUBLANES - 2:SUBLANES - 2 + tm, :]
             + cw_ref[1:2, cols] * stage_ref[slot, SUBLANES - 1:SUBLANES - 1 + tm, :] + cw_ref[2:3, cols] * ua)
        act_ref[:, cols] = (_gelu_tanh(a) * uv).astype(BF16)

    out_ref[...] = _rms(out_ref[...] + _dot(act_ref[...], wd_ref[...]), nf_ref[...])


def _ffn_call(x2, og, of, gate, wg, wf, wo, n2, wu, cw, cb, wd, nf, batch, seq):
    tm = ROW_TILE
    n = seq // tm
    rows = lambda b, i: (b * n + i, 0)
    return pl.pallas_call(
        _ffn_kernel,
        grid=(batch, n),
        in_specs=[
            pl.BlockSpec((tm, D_MODEL), rows),
            pl.BlockSpec((tm, GLA_WIDTH), rows),
            pl.BlockSpec((tm, FOX_WIDTH), rows),
            pl.BlockSpec((tm, 2 * D_MODEL), rows),
            _resident((GLA_WIDTH, D_MODEL)),
            _resident((FOX_WIDTH, D_MODEL)),
            _resident((D_MODEL, D_MODEL)),
            _resident((1, D_MODEL)),
            _resident((D_MODEL, 2 * D_FF)),
            _resident((SUBLANES, D_FF)),
            _resident((1, D_FF)),
            _resident((D_FF, D_MODEL)),
            _resident((1, D_MODEL)),
        ],
        out_specs=pl.BlockSpec((tm, D_MODEL), rows),
        out_shape=jax.ShapeDtypeStruct((batch * seq, D_MODEL), F32),
        scratch_shapes=[
            pltpu.VMEM((SUBLANES, D_FF), F32),
            pltpu.VMEM((2, tm + SUBLANES, FF_CHUNK), F32),
            pltpu.VMEM((tm, D_FF), BF16),
        ],
        compiler_params=_GRID_PARAMS,
        name="ffn",
    )(x2, og, of, gate, wg, wf, wo, n2, wu, cw, cb, wd, nf)


def _layer(x2, batch, seq, norm_mix_g, w_in, gla_alpha_w2, gla_alpha_b, gla_out_norm_g,
           fox_forget_b, fox_q_norm_g, fox_k_norm_g, gate_b, w_gla_branch, w_fox_branch,
           w_out, norm_ffn_g, w_up, conv_w, conv_b, w_down, final_g):
    offs = [0]
    for s in IN_SPLITS:
        offs.append(offs[-1] + s)
    sec = lambda a, b: w_in[:, offs[a]:offs[b]]
    small_pad = LANES - LR_COPIES * GLA_LOWRANK - N_SPLIT * FOX_HEADS
    w_small = jnp.concatenate([sec(4, 5)] * LR_COPIES + [sec(8, 9)] * N_SPLIT
                              + [jnp.zeros((D_MODEL, small_pad), w_in.dtype)], axis=1).astype(BF16)
    w_gla, w_fox, w_gate = sec(0, 4).astype(BF16), sec(5, 8).astype(BF16), sec(9, 11).astype(BF16)
    w2_hi, w2_mid, w2_lo = _split3(gla_alpha_w2)
    a2 = jnp.concatenate([w2_hi, w2_mid, w2_hi, w2_lo, w2_mid, w2_hi,
                          jnp.zeros((LANES - OFF_FF_IN_SMALL, GLA_KW), BF16)], axis=0)
    fb = jnp.concatenate([jnp.zeros((OFF_FF_IN_SMALL,), F32)] + [fox_forget_b] * N_SPLIT
                         + [jnp.zeros((small_pad,), F32)]).reshape(1, LANES)
    qg = (jnp.tile(fox_q_norm_g, FOX_HEADS) * (FOX_HD ** -0.5 * LOG2E)).reshape(1, FOX_WIDTH)
    kg = jnp.tile(fox_k_norm_g, FOX_HEADS).reshape(1, FOX_WIDTH)
    src = jnp.arange(LANES)[:, None] - OFF_FF_IN_SMALL
    col = jnp.arange(FOX_SPREAD)[None, :]
    place = ((src >= 0) & (src < N_SPLIT * FOX_HEADS)
             & (col == (src % FOX_HEADS) * LANES + FOX_HD + src // FOX_HEADS)).astype(BF16)

    gla_in, la, fq, fk, vt, ct, gates = _proj_call(
        x2, norm_mix_g.reshape(1, D_MODEL), w_gla, w_small, w_fox, w_gate, a2, gla_alpha_b.reshape(1, GLA_KW),
        fb, qg, kg, gate_b.reshape(1, 2 * D_MODEL), place, batch, seq)

    o_gla = _gla_call(gla_in, la, gla_out_norm_g.reshape(1, GLA_WIDTH), batch, seq)

    o_fox = _fox_call(fq, fk, vt, ct, batch, seq)

    cw = jnp.pad(conv_w, ((0, SUBLANES - CONV_W), (0, 0)))
    return _ffn_call(
        x2, o_gla, o_fox, gates, w_gla_branch.astype(BF16), w_fox_branch.astype(BF16), w_out.astype(BF16),
        norm_ffn_g.reshape(1, D_MODEL), w_up.astype(BF16), cw, conv_b.reshape(1, D_FF), w_down.astype(BF16),
        final_g.reshape(1, D_MODEL), batch, seq)


def kernel(x, norm_mix_g, w_in, gla_alpha_w2, gla_alpha_b, gla_out_norm_g, fox_forget_b, fox_q_norm_g,
           fox_k_norm_g, gate_b, w_gla_branch, w_fox_branch, w_out, norm_ffn_g, w_up, conv_w, conv_b,
           w_down, norm_final_g):
    batch, seq, _ = x.shape
    assert norm_mix_g.shape[0] == 1, "the final RMSNorm is fused into the single layer's last kernel"
    assert seq % ROW_TILE == 0 and seq % GLA_TILE == 0
    out = _layer(x.reshape(batch * seq, D_MODEL), batch, seq, norm_mix_g[0], w_in[0], gla_alpha_w2[0],
                 gla_alpha_b[0], gla_out_norm_g[0], fox_forget_b[0], fox_q_norm_g[0], fox_k_norm_g[0],
                 gate_b[0], w_gla_branch[0], w_fox_branch[0], w_out[0], norm_ffn_g[0], w_up[0],
                 conv_w[0], conv_b[0], w_down[0], norm_final_g)
    return out.reshape(batch, seq, D_MODEL)
```

```python
import jax
import jax.numpy as jnp
from jax import lax
from jax.experimental import pallas as pl
from jax.experimental.pallas import tpu as pltpu

D_MODEL = 1024
CHUNK = 64
GLA_HEADS = 4
GLA_DK = 64
GLA_DV = 128
GLA_LOWRANK = 16
GLA_TAU = 16.0
FOX_HEADS = 8
FOX_HD = 64
D_FF = 2816
CONV_W = 3
EPS = 1e-6

GLA_KW = GLA_HEADS * GLA_DK
GLA_WIDTH = GLA_HEADS * GLA_DV
FOX_WIDTH = FOX_HEADS * FOX_HD
IN_SPLITS = (GLA_KW, GLA_KW, GLA_WIDTH, GLA_WIDTH, GLA_LOWRANK,
             FOX_WIDTH, FOX_WIDTH, FOX_WIDTH, FOX_HEADS, D_MODEL, D_MODEL)

LANES = 128
SUBLANES = 8
BF16_ROWS = 16
MXU_N = 256
VMEM_LIMIT = 56 * 1024 * 1024

N_SPLIT = 3
LOG2E = 1.4426950408889634

SEC_GLA = 2 * GLA_KW + 2 * GLA_WIDTH
LR_COPIES = 6
OFF_FF_IN_SMALL = LR_COPIES * GLA_LOWRANK
FOX_SPREAD = FOX_HEADS * LANES

ROW_TILE = 512
GLA_TILE = 2048
FF_CHUNK = MXU_N
N_FF_CHUNKS = D_FF // FF_CHUNK
FOX_GROUP = 4
VT_ROWS = FOX_HD + BF16_ROWS

BF16 = jnp.bfloat16
F32 = jnp.float32
NEG = -1e30


def _dot(a, b):
    return jnp.dot(a, b, preferred_element_type=F32)


def _dot_nt(a, b):
    return lax.dot_general(a, b, (((1,), (1,)), ((), ())), preferred_element_type=F32)


def _rms(x, g):
    return x * lax.rsqrt(jnp.mean(x * x, axis=-1, keepdims=True) + EPS) * g


def _log_sigmoid(z):
    return jnp.minimum(z, 0.0) - jnp.log1p(jnp.exp(-jnp.abs(z)))


def _split3(a):
    hi = a.astype(BF16)
    r1 = a - hi.astype(F32)
    mid = r1.astype(BF16)
    lo = (r1 - mid.astype(F32)).astype(BF16)
    return hi, mid, lo


def _cumsum_rows_scan(x, period=None):
    n, w = x.shape
    period = period or n
    row = lax.broadcasted_iota(jnp.int32, (n, w), 0) & (period - 1)
    d = 1
    while d < period:
        if d < SUBLANES:
            shifted = jnp.where(row >= d, pltpu.roll(x, d, axis=0), 0.0)
        else:
            shifted = jnp.concatenate([jnp.zeros((d, w), F32), x[:n - d]], axis=0)
            if period < n:
                shifted = jnp.where(row >= d, shifted, 0.0)
        x = x + shifted
        d *= 2
    return x


def _lower_tri(n):
    r = lax.broadcasted_iota(jnp.int32, (n, n), 0)
    c = lax.broadcasted_iota(jnp.int32, (n, n), 1)
    return r >= c


def _resident(shape):
    return pl.BlockSpec(shape, lambda b, i: (0,) * len(shape), pipeline_mode=pl.Buffered(1))


_GRID_PARAMS = pltpu.CompilerParams(
    dimension_semantics=("arbitrary", "arbitrary"), vmem_limit_bytes=VMEM_LIMIT)


def _proj_kernel(x_ref, g_ref, w_gla_ref, w_small_ref, w_fox_ref, w_gate_ref,
                 a2_ref, ab_ref, fb_ref, qg_ref, kg_ref, gb_ref, place_ref,
                 gla_ref, la_ref, fq_ref, fk_ref, vt_ref, ct_ref, gate_ref, carry_ref):
    tm = x_ref.shape[0]

    @pl.when(pl.program_id(1) == 0)
    def _():
        carry_ref[...] = jnp.zeros_like(carry_ref)

    h = _rms(x_ref[...], g_ref[...]).astype(BF16)

    small = _dot(h, w_small_ref[...])
    gate_ref[...] = jax.nn.sigmoid(_dot(h, w_gate_ref[...]) + gb_ref[...]).astype(BF16)
    lane = lax.broadcasted_iota(jnp.int32, (1, LANES), 1)

    s_hi, s_mid, s_lo = _split3(small)
    grp = lane // GLA_LOWRANK
    glr_terms = jnp.where((grp == 2) | (grp == 4), s_mid, jnp.where(grp == 5, s_lo, s_hi))
    z = _dot(glr_terms, a2_ref[...]) + ab_ref[...]
    la_ref[...] = _log_sigmoid(z) * (1.0 / GLA_TAU)

    lf = _log_sigmoid(small + fb_ref[...])
    c = _cumsum_rows_scan(lf) + carry_ref[...]
    carry_ref[...] = c[tm - 1:tm, :]
    ct_ref[0, 0] = jnp.transpose(c)[OFF_FF_IN_SMALL:OFF_FF_IN_SMALL + FOX_HEADS, :]
    c_hi, c_mid, c_lo = _split3(c * (-LOG2E))
    c_terms = jnp.where(lane < OFF_FF_IN_SMALL + FOX_HEADS, c_hi,
                        jnp.where(lane < OFF_FF_IN_SMALL + 2 * FOX_HEADS, c_mid, c_lo))
    c_place = _dot(c_terms, place_ref[...])

    ones = jnp.where((lane >= FOX_HD) & (lane < FOX_HD + N_SPLIT), 1.0, 0.0)
    low = lane < FOX_HD

    def pair_norm(t, g):
        t2 = t * t
        ms_a = jnp.sum(jnp.where(low, t2, 0.0), axis=-1, keepdims=True) * (1.0 / FOX_HD)
        ms_b = jnp.sum(jnp.where(low, 0.0, t2), axis=-1, keepdims=True) * (1.0 / FOX_HD)
        return t * jnp.where(low, lax.rsqrt(ms_a + EPS), lax.rsqrt(ms_b + EPS)) * g

    fqk = _dot(h, w_fox_ref[:, 0:2 * FOX_WIDTH])
    for pr in range(FOX_HEADS // 2):
        qn = pair_norm(fqk[:, pr * LANES:(pr + 1) * LANES], qg_ref[:, pr * LANES:(pr + 1) * LANES])
        kn = pair_norm(fqk[:, FOX_WIDTH + pr * LANES:FOX_WIDTH + (pr + 1) * LANES],
                       kg_ref[:, pr * LANES:(pr + 1) * LANES])
        for half, (qh, kh) in enumerate(((qn, kn), (pltpu.roll(qn, FOX_HD, axis=1), pltpu.roll(kn, FOX_HD, axis=1)))):
            sl = slice((2 * pr + half) * LANES, (2 * pr + half + 1) * LANES)
            fq_ref[0, 0, sl, :] = jnp.transpose(jnp.where(low, qh, ones)).astype(BF16)
            fk_ref[:, sl] = jnp.where(low, kh, c_place[:, sl]).astype(BF16)

    fvt = jnp.transpose(_dot(h, w_fox_ref[:, 2 * FOX_WIDTH:])).astype(BF16)
    for hd in range(FOX_HEADS):
        vt_ref[0, 0, hd * VT_ROWS:hd * VT_ROWS + FOX_HD, :] = fvt[hd * FOX_HD:(hd + 1) * FOX_HD, :]
        vt_ref[0, 0, hd * VT_ROWS + FOX_HD:(hd + 1) * VT_ROWS, :] = jnp.ones((BF16_ROWS, tm), BF16)

    gla_ref[...] = _dot(h, w_gla_ref[...]).astype(BF16)


def _proj_call(x2, g, w_gla, w_small, w_fox, w_gate, a2, ab, fb, qg, kg, gb, place, batch, seq):
    tm = ROW_TILE
    n = seq // tm
    rows = lambda b, i: (b * n + i, 0)
    tokens = batch * seq
    return pl.pallas_call(
        _proj_kernel,
        grid=(batch, n),
        in_specs=[
            pl.BlockSpec((tm, D_MODEL), rows),
            _resident((1, D_MODEL)),
            _resident((D_MODEL, SEC_GLA)),
            _resident((D_MODEL, LANES)),
            _resident((D_MODEL, 3 * FOX_WIDTH)),
            _resident((D_MODEL, 2 * D_MODEL)),
            _resident((LANES, GLA_KW)),
            _resident((1, GLA_KW)),
            _resident((1, LANES)),
            _resident((1, FOX_WIDTH)),
            _resident((1, FOX_WIDTH)),
            _resident((1, 2 * D_MODEL)),
            _resident((LANES, FOX_SPREAD)),
        ],
        out_specs=[
            pl.BlockSpec((tm, SEC_GLA), rows),
            pl.BlockSpec((tm, GLA_KW), rows),
            pl.BlockSpec((1, 1, FOX_SPREAD, tm), lambda b, i: (b, i, 0, 0)),
            pl.BlockSpec((tm, FOX_SPREAD), rows),
            pl.BlockSpec((1, 1, FOX_HEADS * VT_ROWS, tm), lambda b, i: (b, i, 0, 0)),
            pl.BlockSpec((1, 1, FOX_HEADS, tm), lambda b, i: (b, i, 0, 0)),
            pl.BlockSpec((tm, 2 * D_MODEL), rows),
        ],
        out_shape=[
            jax.ShapeDtypeStruct((tokens, SEC_GLA), BF16),
            jax.ShapeDtypeStruct((tokens, GLA_KW), F32),
            jax.ShapeDtypeStruct((batch, n, FOX_SPREAD, tm), BF16),
            jax.ShapeDtypeStruct((tokens, FOX_SPREAD), BF16),
            jax.ShapeDtypeStruct((batch, n, FOX_HEADS * VT_ROWS, tm), BF16),
            jax.ShapeDtypeStruct((batch, n, FOX_HEADS, tm), F32),
            jax.ShapeDtypeStruct((tokens, 2 * D_MODEL), BF16),
        ],
        scratch_shapes=[pltpu.VMEM((1, LANES), F32)],
        compiler_params=_GRID_PARAMS,
        name="proj",
    )(x2, g, w_gla, w_small, w_fox, w_gate, a2, ab, fb, qg, kg, gb, place)


def _gla_kernel(gin_ref, la_ref, ng_ref, o_ref, st_ref):
    tm = gin_ref.shape[0]
    chunks = range(tm // CHUNK)
    heads = range(GLA_HEADS)

    @pl.when(pl.program_id(1) == 0)
    def _():
        st_ref[...] = jnp.zeros_like(st_ref)

    causal = _lower_tri(CHUNK)
    lane_head = lax.broadcasted_iota(jnp.int32, (1, GLA_KW), 1) // GLA_DK
    ng = ng_ref[...]
    rows = [slice(c * CHUNK, (c + 1) * CHUNK) for c in chunks]
    pair_rows = [slice((c // 2) * 2 * CHUNK, (c // 2 + 1) * 2 * CHUNK) for c in chunks]
    vcol = [slice(2 * GLA_KW + hd * GLA_DV, 2 * GLA_KW + (hd + 1) * GLA_DV) for hd in heads]

    b_all = _cumsum_rows_scan(la_ref[...], period=CHUNK)
    bs = [b_all[rows[c], :] for c in chunks]

    q_e, k_e, k_d, dec = [], [], [], []
    for c in chunks:
        b_last = bs[c][CHUNK - 1:CHUNK, :]
        q = gin_ref[rows[c], 0:GLA_KW].astype(F32) * (GLA_DK ** -0.5)
        k = gin_ref[rows[c], GLA_KW:2 * GLA_KW].astype(F32)
        q_e.append((q * jnp.exp(bs[c])).astype(BF16))
        k_e.append((k * jnp.exp(-bs[c])).astype(BF16))
        k_d.append(k * jnp.exp(b_last - bs[c]))
        dec.append(jnp.transpose(jnp.broadcast_to(jnp.exp(b_last), (LANES, GLA_KW))))

    q_h = [[jnp.where(lane_head == hd, q_e[c], jnp.zeros_like(q_e[c])) for hd in heads] for c in chunks]
    s = [[jnp.where(causal, _dot_nt(q_h[c][hd], k_e[c]), 0.0).astype(BF16) for hd in heads] for c in chunks]
    o_intra = [[_dot(s[c][hd], gin_ref[rows[c], vcol[hd]]) for hd in heads] for c in chunks]

    token_lane = lax.broadcasted_iota(jnp.int32, (1, 2 * CHUNK), 1) // CHUNK
    k_dt = []
    for c in chunks[::2]:
        both = jnp.transpose(jnp.concatenate([k_d[c], k_d[c + 1]], axis=0))
        k_dt += [jnp.where(token_lane == half, both, 0.0).astype(BF16) for half in range(2)]
    kv = [jnp.concatenate([_dot(k_dt[c][hd * GLA_DK:(hd + 1) * GLA_DK, :], gin_ref[pair_rows[c], vcol[hd]])
                           for hd in heads], axis=0) for c in chunks]

    state = st_ref[...]
    states = []
    for c in chunks:
        states.append(state.astype(BF16))
        state = dec[c] * state + kv[c]
    st_ref[...] = state

    for c in chunks:
        r = gin_ref[rows[c], 2 * GLA_KW + GLA_WIDTH:].astype(F32)
        outs = [_rms(o_intra[c][hd] + _dot(q_h[c][hd], states[c]), ng[:, hd * GLA_DV:(hd + 1) * GLA_DV])
                for hd in heads]
        o_ref[rows[c], :] = (jnp.concatenate(outs, axis=1) * (r * jax.nn.sigmoid(r))).astype(o_ref.dtype)


def _gla_call(gin, la, ng, batch, seq):
    tm = GLA_TILE
    n = seq // tm
    rows = lambda b, i: (b * n + i, 0)
    return pl.pallas_call(
        _gla_kernel,
        grid=(batch, n),
        in_specs=[
            pl.BlockSpec((tm, SEC_GLA), rows),
            pl.BlockSpec((tm, GLA_KW), rows),
            pl.BlockSpec((1, GLA_WIDTH), lambda b, i: (0, 0)),
        ],
        out_specs=pl.BlockSpec((tm, GLA_WIDTH), rows),
        out_shape=jax.ShapeDtypeStruct((batch * seq, GLA_WIDTH), BF16),
        scratch_shapes=[pltpu.VMEM((GLA_KW, GLA_DV), F32)],
        compiler_params=_GRID_PARAMS,
        name="gla",
    )(gin, la, ng)


def _fox_kernel(qt_ref, k_ref, vt_ref, ct_ref, o_ref, s_ref, p_ref, acc_ref):
    grp = pl.program_id(1)
    seq = k_ref.shape[0]
    blk = vt_ref.shape[3]
    nq = seq // blk
    heads = range(FOX_GROUP)
    lanes = [slice(j * LANES, (j + 1) * LANES) for j in heads]

    def scores(qi, ki):
        k0 = pl.multiple_of(ki * blk, blk)
        col_max = []
        for j in heads:
            s = _dot(k_ref[pl.ds(k0, blk), lanes[j]], qt_ref[0, qi, lanes[j], :])
            s_ref[j] = s
            col_max.append(jnp.max(s, axis=0, keepdims=True))
        return col_max


    hb = blk // 2
    lo, hi = slice(0, hb), slice(hb, blk)

    def diagonal_probs(cqs, ms):
        tri = (lax.broadcasted_iota(jnp.int32, (hb, hb), 0)
               <= lax.broadcasted_iota(jnp.int32, (hb, hb), 1))
        alphas = []
        for j in heads:
            s00 = jnp.where(tri, s_ref[j, lo, lo], NEG)
            s01 = s_ref[j, lo, hi]
            s11 = jnp.where(tri, s_ref[j, hi, hi], NEG)
            cm = jnp.concatenate(
                [jnp.max(s00, axis=0, keepdims=True),
                 jnp.maximum(jnp.max(s01, axis=0, keepdims=True), jnp.max(s11, axis=0, keepdims=True))], axis=1)
            m_new = jnp.maximum(ms[j], cm + cqs[j])
            ref = m_new - cqs[j]
            p_ref[j, lo, lo] = jnp.exp2(s00 - ref[:, lo]).astype(BF16)
            p_ref[j, lo, hi] = jnp.exp2(s01 - ref[:, hi]).astype(BF16)
            p_ref[j, hi, hi] = jnp.exp2(s11 - ref[:, hi]).astype(BF16)
            alphas.append(jnp.exp2(ms[j] - m_new))
        return alphas

    def diagonal_accumulate(ki, alphas):
        for j in heads:
            vt = vt_ref[0, ki, j * VT_ROWS:(j + 1) * VT_ROWS, :]
            acc_ref[j, :, lo] = alphas[j][:, lo] * acc_ref[j, :, lo] + _dot(vt[:, lo], p_ref[j, lo, lo])
            acc_ref[j, :, hi] = alphas[j][:, hi] * acc_ref[j, :, hi] + _dot(vt, p_ref[j, :, hi])

    def q_body(qi, col_max):
        cqs = [ct_ref[0, qi, pl.ds(FOX_GROUP * grp + j, 1), :] * LOG2E for j in heads]
        acc_ref[...] = jnp.zeros_like(acc_ref)

        def kv_step(ki, state):
            ms, col_max = state
            k0 = pl.multiple_of((ki + 1) * blk, blk)
            new_ms, new_max = [], []
            for j in heads:
                m_new = jnp.maximum(ms[j], col_max[j] + cqs[j])
                p_ref[j] = jnp.exp2(s_ref[j] - (m_new - cqs[j])).astype(BF16)
                s = _dot(k_ref[pl.ds(k0, blk), lanes[j]], qt_ref[0, qi, lanes[j], :])
                s_ref[j] = s
                new_max.append(jnp.max(s, axis=0, keepdims=True))
                acc_ref[j] = (jnp.exp2(ms[j] - m_new) * acc_ref[j]
                              + _dot(vt_ref[0, ki, j * VT_ROWS:(j + 1) * VT_ROWS, :], p_ref[j]))
                new_ms.append(m_new)
            return new_ms, new_max

        ms, _ = lax.fori_loop(0, qi, kv_step, ([jnp.full((1, blk), NEG, F32) for _ in heads], col_max))
        alphas = diagonal_probs(cqs, ms)
        col_max = scores(jnp.minimum(qi + 1, nq - 1), 0)
        diagonal_accumulate(qi, alphas)
        o_t = jnp.concatenate([acc_ref[j, 0:FOX_HD, :] / acc_ref[j, FOX_HD:FOX_HD + 1, :] for j in heads], axis=0)
        o_ref[pl.ds(pl.multiple_of(qi * blk, blk), blk), :] = jnp.transpose(o_t).astype(o_ref.dtype)
        return col_max

    lax.fori_loop(0, nq, q_body, scores(0, 0))


def _fox_call(fq, fk, vt, ct4, batch, seq):
    blk = ROW_TILE
    groups = FOX_HEADS // FOX_GROUP
    return pl.pallas_call(
        _fox_kernel,
        grid=(batch, groups),
        in_specs=[
            pl.BlockSpec((1, seq // blk, FOX_GROUP * LANES, blk), lambda b, p: (b, 0, p, 0)),
            pl.BlockSpec((seq, FOX_GROUP * LANES), lambda b, p: (b, p)),
            pl.BlockSpec((1, seq // blk, FOX_GROUP * VT_ROWS, blk), lambda b, p: (b, 0, p, 0)),
            pl.BlockSpec((1, seq // blk, FOX_HEADS, blk), lambda b, p: (b, 0, 0, 0)),
        ],
        out_specs=pl.BlockSpec((seq, FOX_GROUP * FOX_HD), lambda b, p: (b, p)),
        out_shape=jax.ShapeDtypeStruct((batch * seq, FOX_WIDTH), BF16),
        scratch_shapes=[
            pltpu.VMEM((FOX_GROUP, blk, blk), F32),
            pltpu.VMEM((FOX_GROUP, blk, blk), BF16),
            pltpu.VMEM((FOX_GROUP, VT_ROWS, blk), F32),
        ],
        compiler_params=_GRID_PARAMS,
        name="fox",
    )(fq, fk, vt, ct4)


def _gelu_tanh(a):
    return 0.5 * a * (1.0 + jnp.tanh(0.7978845608028654 * (a + 0.044715 * (a * a * a))))


def _ffn_kernel(x_ref, og_ref, of_ref, gate_ref, wg_ref, wf_ref, wo_ref, n2_ref,
                wu_ref, cw_ref, cb_ref, wd_ref, nf_ref, out_ref,
                carry_ref, stage_ref, act_ref):
    tm = x_ref.shape[0]

    @pl.when(pl.program_id(1) == 0)
    def _():
        carry_ref[...] = jnp.zeros_like(carry_ref)

    y = (gate_ref[:, 0:D_MODEL].astype(F32) * _dot(og_ref[...], wg_ref[...])
         + gate_ref[:, D_MODEL:].astype(F32) * _dot(of_ref[...], wf_ref[...]))
    x1 = x_ref[...] + _dot(y.astype(BF16), wo_ref[...])
    out_ref[...] = x1
    h2 = _rms(x1, n2_ref[...]).astype(BF16)

    for j in range(N_FF_CHUNKS):
        cols = slice(j * FF_CHUNK, (j + 1) * FF_CHUNK)
        ua = _dot(h2, wu_ref[:, cols])
        uv = _dot(h2, wu_ref[:, D_FF + j * FF_CHUNK:D_FF + (j + 1) * FF_CHUNK])
        slot = j % stage_ref.shape[0]
        stage_ref[slot, 0:SUBLANES, :] = carry_ref[:, cols]
        stage_ref[slot, SUBLANES:, :] = ua
        carry_ref[:, cols] = ua[tm - SUBLANES:, :]
        a = (cb_ref[:, cols] + cw_ref[0:1, cols] * stage_ref[slot, SUBLANES - 2:SUBLANES - 2 + tm, :]
             + cw_ref[1:2, cols] * stage_ref[slot, SUBLANES - 1:SUBLANES - 1 + tm, :] + cw_ref[2:3, cols] * ua)
        act_ref[:, cols] = (_gelu_tanh(a) * uv).astype(BF16)

    out_ref[...] = _rms(out_ref[...] + _dot(act_ref[...], wd_ref[...]), nf_ref[...])


def _ffn_call(x2, og, of, gate, wg, wf, wo, n2, wu, cw, cb, wd, nf, batch, seq):
    tm = ROW_TILE
    n = seq // tm
    rows = lambda b, i: (b * n + i, 0)
    return pl.pallas_call(
        _ffn_kernel,
        grid=(batch, n),
        in_specs=[
            pl.BlockSpec((tm, D_MODEL), rows),
            pl.BlockSpec((tm, GLA_WIDTH), rows),
            pl.BlockSpec((tm, FOX_WIDTH), rows),
            pl.BlockSpec((tm, 2 * D_MODEL), rows),
            _resident((GLA_WIDTH, D_MODEL)),
            _resident((FOX_WIDTH, D_MODEL)),
            _resident((D_MODEL, D_MODEL)),
            _resident((1, D_MODEL)),
            _resident((D_MODEL, 2 * D_FF)),
            _resident((SUBLANES, D_FF)),
            _resident((1, D_FF)),
            _resident((D_FF, D_MODEL)),
            _resident((1, D_MODEL)),
        ],
        out_specs=pl.BlockSpec((tm, D_MODEL), rows),
        out_shape=jax.ShapeDtypeStruct((batch * seq, D_MODEL), F32),
        scratch_shapes=[
            pltpu.VMEM((SUBLANES, D_FF), F32),
            pltpu.VMEM((2, tm + SUBLANES, FF_CHUNK), F32),
            pltpu.VMEM((tm, D_FF), BF16),
        ],
        compiler_params=_GRID_PARAMS,
        name="ffn",
    )(x2, og, of, gate, wg, wf, wo, n2, wu, cw, cb, wd, nf)


def _layer(x2, batch, seq, norm_mix_g, w_in, gla_alpha_w2, gla_alpha_b, gla_out_norm_g,
           fox_forget_b, fox_q_norm_g, fox_k_norm_g, gate_b, w_gla_branch, w_fox_branch,
           w_out, norm_ffn_g, w_up, conv_w, conv_b, w_down, final_g):
    offs = [0]
    for s in IN_SPLITS:
        offs.append(offs[-1] + s)
    sec = lambda a, b: w_in[:, offs[a]:offs[b]]
    small_pad = LANES - LR_COPIES * GLA_LOWRANK - N_SPLIT * FOX_HEADS
    w_small = jnp.concatenate([sec(4, 5)] * LR_COPIES + [sec(8, 9)] * N_SPLIT
                              + [jnp.zeros((D_MODEL, small_pad), w_in.dtype)], axis=1).astype(BF16)
    w_gla, w_fox, w_gate = sec(0, 4).astype(BF16), sec(5, 8).astype(BF16), sec(9, 11).astype(BF16)
    w2_hi, w2_mid, w2_lo = _split3(gla_alpha_w2)
    a2 = jnp.concatenate([w2_hi, w2_mid, w2_hi, w2_lo, w2_mid, w2_hi,
                          jnp.zeros((LANES - OFF_FF_IN_SMALL, GLA_KW), BF16)], axis=0)
    fb = jnp.concatenate([jnp.zeros((OFF_FF_IN_SMALL,), F32)] + [fox_forget_b] * N_SPLIT
                         + [jnp.zeros((small_pad,), F32)]).reshape(1, LANES)
    qg = (jnp.tile(fox_q_norm_g, FOX_HEADS) * (FOX_HD ** -0.5 * LOG2E)).reshape(1, FOX_WIDTH)
    kg = jnp.tile(fox_k_norm_g, FOX_HEADS).reshape(1, FOX_WIDTH)
    src = jnp.arange(LANES)[:, None] - OFF_FF_IN_SMALL
    col = jnp.arange(FOX_SPREAD)[None, :]
    place = ((src >= 0) & (src < N_SPLIT * FOX_HEADS)
             & (col == (src % FOX_HEADS) * LANES + FOX_HD + src // FOX_HEADS)).astype(BF16)

    gla_in, la, fq, fk, vt, ct, gates = _proj_call(
        x2, norm_mix_g.reshape(1, D_MODEL), w_gla, w_small, w_fox, w_gate, a2, gla_alpha_b.reshape(1, GLA_KW),
        fb, qg, kg, gate_b.reshape(1, 2 * D_MODEL), place, batch, seq)

    o_gla = _gla_call(gla_in, la, gla_out_norm_g.reshape(1, GLA_WIDTH), batch, seq)

    o_fox = _fox_call(fq, fk, vt, ct, batch, seq)

    cw = jnp.pad(conv_w, ((0, SUBLANES - CONV_W), (0, 0)))
    return _ffn_call(
        x2, o_gla, o_fox, gates, w_gla_branch.astype(BF16), w_fox_branch.astype(BF16), w_out.astype(BF16),
        norm_ffn_g.reshape(1, D_MODEL), w_up.astype(BF16), cw, conv_b.reshape(1, D_FF), w_down.astype(BF16),
        final_g.reshape(1, D_MODEL), batch, seq)


def kernel(x, norm_mix_g, w_in, gla_alpha_w2, gla_alpha_b, gla_out_norm_g, fox_forget_b, fox_q_norm_g,
           fox_k_norm_g, gate_b, w_gla_branch, w_fox_branch, w_out, norm_ffn_g, w_up, conv_w, conv_b,
           w_down, norm_final_g):
    batch, seq, _ = x.shape
    assert norm_mix_g.shape[0] == 1, "the final RMSNorm is fused into the single layer's last kernel"
    assert seq % ROW_TILE == 0 and seq % GLA_TILE == 0
    out = _layer(x.reshape(batch * seq, D_MODEL), batch, seq, norm_mix_g[0], w_in[0], gla_alpha_w2[0],
                 gla_alpha_b[0], gla_out_norm_g[0], fox_forget_b[0], fox_q_norm_g[0], fox_k_norm_g[0],
                 gate_b[0], w_gla_branch[0], w_fox_branch[0], w_out[0], norm_ffn_g[0], w_up[0],
                 conv_w[0], conv_b[0], w_down[0], norm_final_g)
    return out.reshape(batch, seq, D_MODEL)
```
